```python
import jax, jax.numpy as jnp
from jax import lax
import numpy as np

D_MODEL = 1024
BATCH = 16
SEQ = 2048
DEPTH = 1

CHUNK = 64
PREV_CHUNKS = 8
BAND = (PREV_CHUNKS + 1) * CHUNK

HEAD_DIM = 64
RWKV_HEADS = 8
ATT_HEADS = 8
RWKV_WIDTH = RWKV_HEADS * HEAD_DIM
ATT_WIDTH = ATT_HEADS * HEAD_DIM
DECAY_LORA = 64
ICLR_LORA = 64
GATE_LORA = 128
N_BRANCH = 2
REL_CLIP = 128
N_REL = 2 * REL_CLIP + 1

RWKV_COLS = 3 * RWKV_WIDTH + DECAY_LORA + ICLR_LORA + GATE_LORA
ATT_COLS = 3 * ATT_WIDTH
GATE_COLS = N_BRANCH * D_MODEL
IN_COLS = RWKV_COLS + ATT_COLS + GATE_COLS

N_GROUPS = 4
EXPERTS_PER_GROUP = 8
N_EXPERTS = N_GROUPS * EXPERTS_PER_GROUP
TOP_K = 2
D_EXPERT = 512
ROUTE_BLOCK = 128

NORM_EPS = 1e-6
GN_EPS = 64e-5
NEG_INF = -1e30

kernel_name = "hybrid_rwkv7_chunkattn_hiermoe_adaln"


def rms_norm(x, gain):
    xf = x.astype(jnp.float32)
    y = xf * lax.rsqrt(jnp.mean(xf * xf, axis=-1, keepdims=True) + NORM_EPS)
    return (y * gain.astype(jnp.float32)).astype(x.dtype)


def modulate(h, shift, scale):
    return h * (1 + scale[:, None, :]) + shift[:, None, :]


def token_shift(p, mu):
    prev = jnp.pad(p, ((0, 0), (1, 0), (0, 0)))[:, :-1]
    return p + (prev - p) * mu


def rwkv7_branch(p, mu, w0, w_up, a0, a_up, g_up, k_k, k_a, r_k, lnx_g, lnx_b):
    dt = p.dtype
    Bn, S, _ = p.shape
    f32 = jnp.float32
    p = token_shift(p.astype(f32), mu.astype(f32))
    o1, o2, o3 = RWKV_WIDTH, 2 * RWKV_WIDTH, 3 * RWKV_WIDTH
    o4, o5 = o3 + DECAY_LORA, o3 + DECAY_LORA + ICLR_LORA
    r, k, v = p[..., :o1], p[..., o1:o2], p[..., o2:o3]
    wd, ad, gd = p[..., o3:o4], p[..., o4:o5], p[..., o5:]
    w = -jax.nn.softplus(-(w0.astype(f32) + jnp.tanh(wd) @ w_up.astype(f32))) - 0.5
    decay = jnp.exp(-jnp.exp(w))
    a = jax.nn.sigmoid(a0.astype(f32) + ad @ a_up.astype(f32))
    g = jax.nn.sigmoid(gd) @ g_up.astype(f32)

    def heads(t):
        return t.reshape(Bn, S, RWKV_HEADS, HEAD_DIM)

    r, k, v, decay, a = heads(r), heads(k), heads(v), heads(decay), heads(a)
    kk = k * k_k.astype(f32).reshape(RWKV_HEADS, HEAD_DIM)
    kk = kk / jnp.maximum(jnp.sqrt(jnp.sum(kk * kk, axis=-1, keepdims=True)), 1e-12)
    k = k * (1 + (a - 1) * k_a.astype(f32).reshape(RWKV_HEADS, HEAD_DIM))

    def step(state, inp):
        r_t, w_t, k_t, v_t, a_t, b_t = inp
        sa = jnp.einsum('bhvk,bhk->bhv', state, a_t)
        state = (state * w_t[:, :, None, :] + sa[..., None] * b_t[:, :, None, :]
                 + v_t[..., None] * k_t[:, :, None, :])
        return state, jnp.einsum('bhvk,bhk->bhv', state, r_t)

    tm = lambda t: jnp.swapaxes(t, 0, 1)
    state0 = jnp.zeros((Bn, RWKV_HEADS, HEAD_DIM, HEAD_DIM), f32)
    _, y = lax.scan(step, state0, (tm(r), tm(decay), tm(k), tm(v), tm(-kk), tm(kk * a)))
    y = tm(y)
    mean = jnp.mean(y, axis=-1, keepdims=True)
    var = jnp.mean(jnp.square(y - mean), axis=-1, keepdims=True)
    y = ((y - mean) * lax.rsqrt(var + GN_EPS)).reshape(Bn, S, RWKV_WIDTH)
    y = y * lnx_g.astype(f32) + lnx_b.astype(f32)
    bonus = jnp.sum(r * k * r_k.astype(f32), axis=-1, keepdims=True) * v
    y = (y + bonus.reshape(Bn, S, RWKV_WIDTH)) * g
    return y.astype(dt)


def chunk_attention_branch(p, q_g, k_g, rel_bias):
    dt = p.dtype
    Bn, S, _ = p.shape
    f32 = jnp.float32
    pf = p.astype(f32)

    def heads(t):
        return t.reshape(Bn, S, ATT_HEADS, HEAD_DIM).transpose(0, 2, 1, 3)

    q = heads(pf[..., :ATT_WIDTH])
    k = heads(pf[..., ATT_WIDTH:2 * ATT_WIDTH])
    v = heads(pf[..., 2 * ATT_WIDTH:])
    q = rms_norm(q, q_g) * (HEAD_DIM ** -0.5)
    k = rms_norm(k, k_g)
    pad = ((0, 0), (0, 0), (PREV_CHUNKS * CHUNK, 0), (0, 0))
    kp = jnp.pad(k, pad)
    vp = jnp.pad(v, pad)
    q_off = jnp.arange(CHUNK)[:, None]
    k_off = jnp.arange(BAND) - PREV_CHUNKS * CHUNK
    rel_idx = jnp.clip(k_off[None, :] - q_off, -REL_CLIP, REL_CLIP) + REL_CLIP
    bias = rel_bias.astype(f32)[:, rel_idx]

    def one_chunk(ci):
        start = ci * CHUNK
        qc = lax.dynamic_slice_in_dim(q, start, CHUNK, axis=2)
        kb = lax.dynamic_slice_in_dim(kp, start, BAND, axis=2)
        vb = lax.dynamic_slice_in_dim(vp, start, BAND, axis=2)
        s = jnp.einsum('bhqd,bhkd->bhqk', qc, kb) + bias
        valid = (start + k_off) >= 0
        s = jnp.where(valid, s, NEG_INF)
        return jnp.einsum('bhqk,bhkd->bhqd', jax.nn.softmax(s, axis=-1), vb)

    o = lax.map(one_chunk, jnp.arange(S // CHUNK))
    o = o.transpose(1, 0, 3, 2, 4).reshape(Bn, S, ATT_WIDTH)
    return o.astype(dt)


def hier_moe(h, wc, bc, wf, bf, w_gate, w_up, w_down):
    Bn, S, D = h.shape
    f32 = jnp.float32
    n_tok = Bn * S
    xt = h.reshape(n_tok, D)
    xf = xt.astype(f32)
    coarse_p = jax.nn.softmax(xf @ wc.astype(f32) + bc.astype(f32), axis=-1)
    grp = jnp.argmax(coarse_p, axis=-1)
    p_grp = jnp.take_along_axis(coarse_p, grp[:, None], axis=-1)
    fine = (xf @ wf.astype(f32) + bf.astype(f32)).reshape(n_tok, N_GROUPS, EXPERTS_PER_GROUP)
    fine = jnp.take_along_axis(fine, grp[:, None, None], axis=1)[:, 0]
    top_v, top_i = lax.top_k(jax.nn.softmax(fine, axis=-1), TOP_K)
    weights = p_grp * top_v / jnp.sum(top_v, axis=-1, keepdims=True)
    eid = (grp[:, None] * EXPERTS_PER_GROUP + top_i).reshape(-1)

    n_asg = n_tok * TOP_K
    order = jnp.argsort(eid)
    e_sorted = eid[order]
    counts = jnp.bincount(eid, length=N_EXPERTS)
    padded = (counts + ROUTE_BLOCK - 1) // ROUTE_BLOCK * ROUTE_BLOCK
    starts = jnp.cumsum(counts) - counts
    pad_ends = jnp.cumsum(padded)
    pad_starts = pad_ends - padded
    dest = pad_starts[e_sorted] + jnp.arange(n_asg) - starts[e_sorted]
    n_rows = n_asg + N_EXPERTS * ROUTE_BLOCK
    n_blocks = n_rows // ROUTE_BLOCK
    buf = jnp.zeros((n_rows, D), xt.dtype).at[dest].set(xt[order // TOP_K])
    block_e = jnp.minimum(
        jnp.searchsorted(pad_ends, jnp.arange(n_blocks) * ROUTE_BLOCK, side='right'), N_EXPERTS - 1)

    def expert_block(args):
        xb, e = args
        hid = jax.nn.silu(xb @ w_gate[e]) * (xb @ w_up[e])
        return hid @ w_down[e]

    yb = lax.map(expert_block, (buf.reshape(n_blocks, ROUTE_BLOCK, D), block_e)).reshape(n_rows, D)
    y_asg = jnp.zeros((n_asg, D), yb.dtype).at[order].set(yb[dest])
    out = jnp.einsum('nkd,nk->nd', y_asg.reshape(n_tok, TOP_K, D).astype(f32), weights)
    return out.astype(h.dtype).reshape(Bn, S, D)


def setup_inputs(seed: int = 0) -> dict:
    key = jax.random.key(seed)
    ks = iter(jax.random.split(key, 40))
    f32 = jnp.float32
    L = DEPTH

    def nrm(shape, scale):
        return jax.random.normal(next(ks), shape, f32) * scale

    return {
        "x": nrm((BATCH, SEQ, D_MODEL), 1.0),
        "c": nrm((BATCH, D_MODEL), 1.0),
        "w_ada": nrm((L, D_MODEL, 6 * D_MODEL), 0.5 * D_MODEL ** -0.5),
        "b_ada": nrm((L, 6 * D_MODEL), 0.02),
        "norm1_g": 1.0 + nrm((L, D_MODEL), 0.05),
        "w_in": nrm((L, D_MODEL, IN_COLS), D_MODEL ** -0.5),
        "rwkv_mu": jax.random.uniform(next(ks), (L, RWKV_COLS), f32),
        "rwkv_w0": nrm((L, RWKV_WIDTH), 0.5),
        "rwkv_w_up": nrm((L, DECAY_LORA, RWKV_WIDTH), 0.3 * DECAY_LORA ** -0.5),
        "rwkv_a0": nrm((L, RWKV_WIDTH), 0.3),
        "rwkv_a_up": nrm((L, ICLR_LORA, RWKV_WIDTH), 0.3 * ICLR_LORA ** -0.5),
        "rwkv_g_up": nrm((L, GATE_LORA, RWKV_WIDTH), GATE_LORA ** -0.5),
        "rwkv_k_k": 0.85 + nrm((L, RWKV_WIDTH), 0.05),
        "rwkv_k_a": 1.0 + nrm((L, RWKV_WIDTH), 0.05),
        "rwkv_r_k": nrm((L, RWKV_HEADS, HEAD_DIM), 0.1),
        "rwkv_lnx_g": 1.0 + nrm((L, RWKV_WIDTH), 0.05),
        "rwkv_lnx_b": nrm((L, RWKV_WIDTH), 0.02),
        "attn_q_g": 1.0 + nrm((L, HEAD_DIM), 0.05),
        "attn_k_g": 1.0 + nrm((L, HEAD_DIM), 0.05),
        "attn_rel_bias": nrm((L, ATT_HEADS, N_REL), 0.5),
        "w_branch_rwkv": nrm((L, RWKV_WIDTH, D_MODEL), RWKV_WIDTH ** -0.5),
        "w_branch_attn": nrm((L, ATT_WIDTH, D_MODEL), ATT_WIDTH ** -0.5),
        "w_out": nrm((L, D_MODEL, D_MODEL), D_MODEL ** -0.5),
        "norm2_g": 1.0 + nrm((L, D_MODEL), 0.05),
        "router_coarse_w": nrm((L, D_MODEL, N_GROUPS), D_MODEL ** -0.5),
        "router_coarse_b": nrm((L, N_GROUPS), 0.01),
        "router_fine_w": nrm((L, D_MODEL, N_EXPERTS), D_MODEL ** -0.5),
        "router_fine_b": nrm((L, N_EXPERTS), 0.01),
        "expert_w_gate": nrm((L, N_EXPERTS, D_MODEL, D_EXPERT), D_MODEL ** -0.5),
        "expert_w_up": nrm((L, N_EXPERTS, D_MODEL, D_EXPERT), D_MODEL ** -0.5),
        "expert_w_down": nrm((L, N_EXPERTS, D_EXPERT, D_MODEL), D_EXPERT ** -0.5),
    }


def reference(x, c, w_ada, b_ada, norm1_g, w_in, rwkv_mu, rwkv_w0, rwkv_w_up, rwkv_a0,
              rwkv_a_up, rwkv_g_up, rwkv_k_k, rwkv_k_a, rwkv_r_k, rwkv_lnx_g, rwkv_lnx_b,
              attn_q_g, attn_k_g, attn_rel_bias, w_branch_rwkv, w_branch_attn, w_out, norm2_g,
              router_coarse_w, router_coarse_b, router_fine_w, router_fine_b,
              expert_w_gate, expert_w_up, expert_w_down):
    for l in range(DEPTH):
        mod = jax.nn.silu(c) @ w_ada[l] + b_ada[l]
        sh1, sc1, g1, sh2, sc2, g2 = jnp.split(mod, 6, axis=-1)

        h = modulate(rms_norm(x, norm1_g[l]), sh1, sc1)
        p = h @ w_in[l]
        p_rwkv = p[..., :RWKV_COLS]
        p_att = p[..., RWKV_COLS:RWKV_COLS + ATT_COLS]
        p_gate = p[..., RWKV_COLS + ATT_COLS:]
        y_r = rwkv7_branch(p_rwkv, rwkv_mu[l], rwkv_w0[l], rwkv_w_up[l], rwkv_a0[l], rwkv_a_up[l],
                           rwkv_g_up[l], rwkv_k_k[l], rwkv_k_a[l], rwkv_r_k[l],
                           rwkv_lnx_g[l], rwkv_lnx_b[l]) @ w_branch_rwkv[l]
        y_a = chunk_attention_branch(p_att, attn_q_g[l], attn_k_g[l],
                                     attn_rel_bias[l]) @ w_branch_attn[l]
        gate_r = jax.nn.sigmoid(p_gate[..., :D_MODEL])
        gate_a = jax.nn.sigmoid(p_gate[..., D_MODEL:])
        mixed = (gate_r * y_r + gate_a * y_a) @ w_out[l]
        x = x + g1[:, None, :] * mixed

        h2 = modulate(rms_norm(x, norm2_g[l]), sh2, sc2)
        x = x + g2[:, None, :] * hier_moe(h2, router_coarse_w[l], router_coarse_b[l],
                                          router_fine_w[l], router_fine_b[l],
                                          expert_w_gate[l], expert_w_up[l], expert_w_down[l])
    return x
```

```python
import functools

import numpy as np
import jax
import jax.numpy as jnp
from jax import lax
from jax.experimental import pallas as pl
from jax.experimental.pallas import tpu as pltpu

f32 = jnp.float32
bf16 = jnp.bfloat16
HI = lax.Precision.HIGHEST

D_MODEL = 1024
HEAD_DIM = 64
N_HEADS = 8
WIDTH = N_HEADS * HEAD_DIM
DECAY_LORA = 64
ICLR_LORA = 64
GATE_LORA = 128
RWKV_COLS = 3 * WIDTH + DECAY_LORA + ICLR_LORA + GATE_LORA
ATT_COLS = 3 * WIDTH
GATE_COLS = 2 * D_MODEL
CHUNK = 64
PREV_CHUNKS = 8
REL_CLIP = 128
N_GROUPS = 4
EXPERTS_PER_GROUP = 8
N_EXPERTS = N_GROUPS * EXPERTS_PER_GROUP
D_EXPERT = 512
NORM_EPS = 1e-6
GN_EPS = 64e-5
NEG_INF = -1e30

INPROJ_ROWS = 256
RWKV_CHUNK = 64
ATT_QROWS = 128
ATT_WINDOW = PREV_CHUNKS * CHUNK + ATT_QROWS
MERGE_ROWS = 512
MOE_ROWS = 128
COMBINE_ROWS = 256
ROUTE_LANES = 128
VMEM_LIMIT = 56 * 1024 * 1024


def _cparams(sem):
    return pltpu.CompilerParams(dimension_semantics=sem, vmem_limit_bytes=VMEM_LIMIT)


def _sigmoid(x):
    return 1.0 / (1.0 + jnp.exp(-x))


def _ada_kernel(c_ref, w_ref, b_ref, o_ref):
    c = c_ref[...]
    s = c * _sigmoid(c)
    o_ref[...] = jnp.dot(s, w_ref[...], precision=HI, preferred_element_type=f32) + b_ref[...]


def _ada(c, w, b):
    bn, d = c.shape
    n = w.shape[1]
    tn = 1024
    return pl.pallas_call(
        _ada_kernel,
        grid=(n // tn,),
        in_specs=[pl.BlockSpec((bn, d), lambda j: (0, 0)),
                  pl.BlockSpec((d, tn), lambda j: (0, j)),
                  pl.BlockSpec((1, tn), lambda j: (0, j))],
        out_specs=pl.BlockSpec((bn, tn), lambda j: (0, j)),
        out_shape=jax.ShapeDtypeStruct((bn, n), f32),
        compiler_params=_cparams(("arbitrary",)),
        name="ada",
    )(c, w, b.reshape(1, n))


def _rms_mod(x, gain, shift, scale):
    ms = jnp.mean(x * x, axis=-1, keepdims=True)
    h = x * lax.rsqrt(ms + NORM_EPS) * gain
    return h * (1.0 + scale) + shift


def _inproj_kernel(x_ref, g_ref, sh_ref, sc_ref, w_ref, pr_ref, pa_ref, pg_ref):
    h = _rms_mod(x_ref[0], g_ref[...], sh_ref[0], sc_ref[0]).astype(bf16)
    c1, c2 = RWKV_COLS, RWKV_COLS + ATT_COLS
    pr_ref[0] = jnp.dot(h, w_ref[:, :c1], preferred_element_type=f32)
    pa_ref[0] = jnp.dot(h, w_ref[:, c1:c2], preferred_element_type=f32)
    pg_ref[0] = _sigmoid(jnp.dot(h, w_ref[:, c2:], preferred_element_type=f32))


def _inproj(x, gain, shift, scale, w_bf):
    bn, s, d = x.shape
    tm = min(INPROJ_ROWS, s)
    n = w_bf.shape[1]
    row = lambda b, i: (b, i, 0)
    per_b = lambda b, i: (b, 0, 0)
    return pl.pallas_call(
        _inproj_kernel,
        grid=(bn, s // tm),
        in_specs=[pl.BlockSpec((1, tm, d), row),
                  pl.BlockSpec((1, d), lambda b, i: (0, 0)),
                  pl.BlockSpec((1, 1, d), per_b),
                  pl.BlockSpec((1, 1, d), per_b),
                  pl.BlockSpec((d, n), lambda b, i: (0, 0))],
        out_specs=[pl.BlockSpec((1, tm, RWKV_COLS), row),
                   pl.BlockSpec((1, tm, ATT_COLS), row),
                   pl.BlockSpec((1, tm, GATE_COLS), row)],
        out_shape=[jax.ShapeDtypeStruct((bn, s, RWKV_COLS), f32),
                   jax.ShapeDtypeStruct((bn, s, ATT_COLS), f32),
                   jax.ShapeDtypeStruct((bn, s, GATE_COLS), f32)],
        compiler_params=_cparams(("parallel", "parallel")),
        name="inproj",
    )(x, gain.reshape(1, d), shift.reshape(bn, 1, d), scale.reshape(bn, 1, d), w_bf)


def _mm(a, b):
    return jnp.dot(a, b, precision=HI, preferred_element_type=f32)


def _mm_nt(a, b):
    return lax.dot_general(a, b, (((1,), (1,)), ((), ())), precision=HI,
                           preferred_element_type=f32)


def _mm_tn(a, b):
    return lax.dot_general(a, b, (((0,), (0,)), ((), ())), precision=HI,
                           preferred_element_type=f32)


def _rwkv_kernel(p_ref, mu_ref, w0_ref, wup_ref, a0_ref, aup_ref, gup_ref, kk_ref, ka_ref,
                 rk_ref, lg_ref, lb_ref, o_ref, last_ref, st_ref):
    C = RWKV_CHUNK
    c_idx = pl.program_id(1)

    @pl.when(c_idx == 0)
    def _():
        last_ref[...] = jnp.zeros_like(last_ref)
        st_ref[...] = jnp.zeros_like(st_ref)

    pc = p_ref[0]
    row_full = lax.broadcasted_iota(jnp.int32, pc.shape, 0)
    prev = jnp.where(row_full == 0, last_ref[...], pltpu.roll(pc, 1, axis=0))
    last_ref[...] = pc[C - 1:C, :]
    ps = pc + (prev - pc) * mu_ref[...]

    o1, o2, o3 = WIDTH, 2 * WIDTH, 3 * WIDTH
    o4, o5 = o3 + DECAY_LORA, o3 + DECAY_LORA + ICLR_LORA
    r, k, v = ps[:, :o1], ps[:, o1:o2], ps[:, o2:o3]
    wd, ad, gd = ps[:, o3:o4], ps[:, o4:o5], ps[:, o5:]

    wraw = w0_ref[...] + _mm(jnp.tanh(wd), wup_ref[...])
    z = -wraw
    softplus = jnp.maximum(z, 0.0) + jnp.log(1.0 + jnp.exp(-jnp.abs(z)))
    logdec = -jnp.exp(-softplus - 0.5)
    a = _sigmoid(a0_ref[...] + _mm(ad, aup_ref[...]))
    g = _mm(_sigmoid(gd), gup_ref[...])
    kkr = k * kk_ref[...]
    k2 = k * (1.0 + (a - 1.0) * ka_ref[...])
    rkk = r * k2 * rk_ref[...]

    ri = lax.broadcasted_iota(jnp.int32, (C, C), 0)
    ci = lax.broadcasted_iota(jnp.int32, (C, C), 1)
    incl = ri >= ci
    strict = ri > ci
    eye = (ri == ci).astype(f32)
    cum = _mm(incl.astype(f32), logdec)
    e_pos = jnp.exp(cum)
    e_prev = jnp.exp(cum - logdec)
    e_neg = jnp.exp(-cum)
    cum_end = cum[C - 1:C, :]
    e_end = jnp.exp(cum_end - cum)
    g_end = jnp.exp(cum_end)

    outs = []
    for h in range(N_HEADS):
        sl = slice(h * HEAD_DIM, (h + 1) * HEAD_DIM)
        kkh = kkr[:, sl]
        nrm = jnp.sqrt(jnp.sum(kkh * kkh, axis=-1, keepdims=True))
        kkh = kkh / jnp.maximum(nrm, 1e-12)
        rh, kh, vh, ah = r[:, sl], k2[:, sl], v[:, sl], a[:, sl]
        bh = kkh * ah
        at = -kkh * e_prev[:, sl]
        rt = rh * e_pos[:, sl]
        bt = bh * e_neg[:, sl]
        kt = kh * e_neg[:, sl]
        bhat = bh * e_end[:, sl]
        khat = kh * e_end[:, sl]

        nmat = jnp.where(strict, _mm_nt(at, bt), 0.0)
        aak = jnp.where(strict, _mm_nt(at, kt), 0.0)
        arb = jnp.where(incl, _mm_nt(rt, bt), 0.0)
        ark = jnp.where(incl, _mm_nt(rt, kt), 0.0)

        tinv = eye + nmat
        pw = nmat
        for _ in range(5):
            pw = _mm(pw, pw)
            tinv = tinv + _mm(tinv, pw)

        m1 = _mm(tinv, at)
        u0 = _mm(tinv, _mm(aak, vh))
        m2 = rt + _mm(arb, m1)
        y1 = _mm(arb, u0) + _mm(ark, vh)
        gmat = eye * g_end[:, sl] + _mm_tn(bhat, m1)
        dmat = _mm_tn(bhat, u0) + _mm_tn(khat, vh)

        h0 = st_ref[h]
        y = _mm(m2, h0) + y1
        st_ref[h] = _mm(gmat, h0) + dmat

        mean = jnp.mean(y, axis=-1, keepdims=True)
        yc = y - mean
        var = jnp.mean(yc * yc, axis=-1, keepdims=True)
        yn = yc * lax.rsqrt(var + GN_EPS) * lg_ref[:, sl] + lb_ref[:, sl]
        bonus = jnp.sum(rkk[:, sl], axis=-1, keepdims=True) * vh
        outs.append((yn + bonus) * g[:, sl])
    o_ref[0] = jnp.concatenate(outs, axis=-1)


def _rwkv(p_rwkv, mu, w0, w_up, a0, a_up, g_up, k_k, k_a, r_k, lnx_g, lnx_b):
    bn, s, _ = p_rwkv.shape
    C = RWKV_CHUNK
    vec = lambda t: t.reshape(1, -1)
    const = lambda b, c: (0, 0)
    full = lambda t: pl.BlockSpec(t.shape, const)
    args = [vec(mu), vec(w0), w_up, vec(a0), a_up, g_up, vec(k_k), vec(k_a), vec(r_k),
            vec(lnx_g), vec(lnx_b)]
    return pl.pallas_call(
        _rwkv_kernel,
        grid=(bn, s // C),
        in_specs=[pl.BlockSpec((1, C, RWKV_COLS), lambda b, c: (b, c, 0))] + [full(t) for t in args],
        out_specs=pl.BlockSpec((1, C, WIDTH), lambda b, c: (b, c, 0)),
        out_shape=jax.ShapeDtypeStruct((bn, s, WIDTH), f32),
        scratch_shapes=[pltpu.VMEM((1, RWKV_COLS), f32),
                        pltpu.VMEM((N_HEADS, HEAD_DIM, HEAD_DIM), f32)],
        compiler_params=_cparams(("parallel", "arbitrary")),
        name="rwkv",
    )(p_rwkv, *args)


def _head_ms(x, ones_bd):
    return jnp.dot(x * x, ones_bd, precision=HI, preferred_element_type=f32) * (1.0 / HEAD_DIM)


def _attn_kernel(q_ref, k_ref, v_ref, bias_ref, qg_ref, kg_ref, o_ref, kpad_ref, vpad_ref):
    TQ = ATT_QROWS
    PAD = PREV_CHUNKS * CHUNK
    t = pl.program_id(2)
    r2 = lax.broadcasted_iota(jnp.int32, (2 * HEAD_DIM, 2 * HEAD_DIM), 0) // HEAD_DIM
    c2 = lax.broadcasted_iota(jnp.int32, (2 * HEAD_DIM, 2 * HEAD_DIM), 1) // HEAD_DIM
    ones_bd = (r2 == c2).astype(f32)

    @pl.when(t == 0)
    def _():
        kf = k_ref[0]
        kn = kf * lax.rsqrt(_head_ms(kf, ones_bd) + NORM_EPS) * kg_ref[...]
        kpad_ref[:PAD, :] = jnp.zeros((PAD, 2 * HEAD_DIM), bf16)
        vpad_ref[:PAD, :] = jnp.zeros((PAD, 2 * HEAD_DIM), bf16)
        kpad_ref[PAD:, :] = kn.astype(bf16)
        vpad_ref[PAD:, :] = v_ref[0].astype(bf16)

    qf = q_ref[0]
    qn = qf * lax.rsqrt(_head_ms(qf, ones_bd) + NORM_EPS) * (qg_ref[...] * HEAD_DIM ** -0.5)
    qn = qn.astype(bf16)
    start = pl.multiple_of(t * TQ, TQ)
    kw = kpad_ref[pl.ds(start, ATT_WINDOW), :]
    vw = vpad_ref[pl.ds(start, ATT_WINDOW), :]
    kpos = lax.broadcasted_iota(jnp.int32, (TQ, ATT_WINDOW), 1) + (t * TQ - PAD)
    outs = []
    for hh in range(2):
        sl = slice(hh * HEAD_DIM, (hh + 1) * HEAD_DIM)
        s = lax.dot_general(qn[:, sl], kw[:, sl], (((1,), (1,)), ((), ())),
                            preferred_element_type=f32)
        s = jnp.where(kpos >= 0, s + bias_ref[hh], NEG_INF)
        m = jnp.max(s, axis=-1, keepdims=True)
        e = jnp.exp(s - m)
        l = jnp.sum(e, axis=-1, keepdims=True)
        o = jnp.dot(e.astype(bf16), vw[:, sl], preferred_element_type=f32)
        outs.append(o / l)
    o_ref[0] = jnp.concatenate(outs, axis=-1)


def _attn_bias_table(rel_bias):
    pad = PREV_CHUNKS * CHUNK
    qi = np.arange(ATT_QROWS)[:, None]
    koff = np.arange(ATT_WINDOW)[None, :] - pad
    rel_idx = np.clip(koff - qi, -REL_CLIP, REL_CLIP) + REL_CLIP
    cq = qi // CHUNK
    kc = np.floor_divide(koff, CHUNK)
    visible = (kc >= cq - PREV_CHUNKS) & (kc <= cq)
    bias = rel_bias.astype(f32)[:, rel_idx]
    return jnp.where(jnp.asarray(visible)[None], bias, NEG_INF)


def _attn(p_att, q_g, k_g, rel_bias):
    bn, s, _ = p_att.shape
    TQ = ATT_QROWS
    hp = N_HEADS // 2
    pad = PREV_CHUNKS * CHUNK
    bias = _attn_bias_table(rel_bias)
    g2 = lambda t: jnp.tile(t.reshape(1, HEAD_DIM), (1, 2))
    return pl.pallas_call(
        _attn_kernel,
        grid=(bn, hp, s // TQ),
        in_specs=[pl.BlockSpec((1, TQ, 2 * HEAD_DIM), lambda b, j, t: (b, t, j)),
                  pl.BlockSpec((1, s, 2 * HEAD_DIM), lambda b, j, t: (b, 0, hp + j)),
                  pl.BlockSpec((1, s, 2 * HEAD_DIM), lambda b, j, t: (b, 0, 2 * hp + j)),
                  pl.BlockSpec((2, TQ, ATT_WINDOW), lambda b, j, t: (j, 0, 0)),
                  pl.BlockSpec((1, 2 * HEAD_DIM), lambda b, j, t: (0, 0)),
                  pl.BlockSpec((1, 2 * HEAD_DIM), lambda b, j, t: (0, 0))],
        out_specs=pl.BlockSpec((1, TQ, 2 * HEAD_DIM), lambda b, j, t: (b, t, j)),
        out_shape=jax.ShapeDtypeStruct((bn, s, WIDTH), f32),
        scratch_shapes=[pltpu.VMEM((pad + s, 2 * HEAD_DIM), bf16),
                        pltpu.VMEM((pad + s, 2 * HEAD_DIM), bf16)],
        compiler_params=_cparams(("parallel", "parallel", "arbitrary")),
        name="attn",
    )(p_att, p_att, p_att, bias, g2(q_g), g2(k_g))


def _route(logits):
    lane = lax.broadcasted_iota(jnp.int32, logits.shape, 1)
    big = jnp.int32(ROUTE_LANES)
    is_c = lane < N_GROUPS
    cl = jnp.where(is_c, logits, NEG_INF)
    cm = jnp.max(cl, axis=-1, keepdims=True)
    grp = jnp.min(jnp.where(cl == cm, lane, big), axis=-1, keepdims=True)
    p_grp = 1.0 / jnp.sum(jnp.where(is_c, jnp.exp(cl - cm), 0.0), axis=-1, keepdims=True)
    lo = N_GROUPS + grp * EXPERTS_PER_GROUP
    in_g = (lane >= lo) & (lane < lo + EXPERTS_PER_GROUP)
    fl = jnp.where(in_g, logits, NEG_INF)
    f1 = jnp.max(fl, axis=-1, keepdims=True)
    i1 = jnp.min(jnp.where(in_g & (fl == f1), lane, big), axis=-1, keepdims=True)
    fl2 = jnp.where(lane == i1, NEG_INF, fl)
    in_g2 = in_g & (lane != i1)
    f2 = jnp.max(fl2, axis=-1, keepdims=True)
    i2 = jnp.min(jnp.where(in_g2 & (fl2 == f2), lane, big), axis=-1, keepdims=True)
    e2 = jnp.exp(f2 - f1)
    w1 = p_grp / (1.0 + e2)
    w2 = p_grp * e2 / (1.0 + e2)
    out = jnp.where(lane == 0, w1, 0.0)
    out = jnp.where(lane == 1, w2, out)
    out = jnp.where(lane == 2, (i1 - N_GROUPS).astype(f32), out)
    out = jnp.where(lane == 3, (i2 - N_GROUPS).astype(f32), out)
    return out


def _merge_kernel(yr_ref, ya_ref, gt_ref, x_ref, g1_ref, sh_ref, sc_ref, ng_ref, wbr_ref,
                  wba_ref, wout_ref, wrt_ref, brt_ref, x1_ref, h2_ref, route_ref):
    t1 = jnp.dot(yr_ref[0].astype(bf16), wbr_ref[...], preferred_element_type=f32)
    t2 = jnp.dot(ya_ref[0].astype(bf16), wba_ref[...], preferred_element_type=f32)
    gt = gt_ref[0]
    m = gt[:, :D_MODEL] * t1 + gt[:, D_MODEL:] * t2
    mixed = jnp.dot(m.astype(bf16), wout_ref[...], preferred_element_type=f32)
    x1 = x_ref[0] + g1_ref[0] * mixed
    x1_ref[0] = x1
    h2 = _rms_mod(x1, ng_ref[...], sh_ref[0], sc_ref[0])
    h2_ref[0] = h2
    logits = jnp.dot(h2, wrt_ref[...], precision=HI, preferred_element_type=f32) + brt_ref[...]
    route_ref[0] = _route(logits)


def _merge(y_r, y_a, gates, x, g1, sh2, sc2, norm2_g, w_br, w_ba, w_out, w_rt, b_rt):
    bn, s, d = x.shape
    tm = min(MERGE_ROWS, s)
    row = lambda b, i: (b, i, 0)
    per_b = lambda b, i: (b, 0, 0)
    const = lambda b, i: (0, 0)
    full = lambda t: pl.BlockSpec(t.shape, const)
    return pl.pallas_call(
        _merge_kernel,
        grid=(bn, s // tm),
        in_specs=[pl.BlockSpec((1, tm, WIDTH), row),
                  pl.BlockSpec((1, tm, WIDTH), row),
                  pl.BlockSpec((1, tm, GATE_COLS), row),
                  pl.BlockSpec((1, tm, d), row),
                  pl.BlockSpec((1, 1, d), per_b),
                  pl.BlockSpec((1, 1, d), per_b),
                  pl.BlockSpec((1, 1, d), per_b),
                  pl.BlockSpec((1, d), const),
                  full(w_br), full(w_ba), full(w_out), full(w_rt), full(b_rt)],
        out_specs=[pl.BlockSpec((1, tm, d), row),
                   pl.BlockSpec((1, tm, d), row),
                   pl.BlockSpec((1, tm, ROUTE_LANES), row)],
        out_shape=[jax.ShapeDtypeStruct((bn, s, d), f32),
                   jax.ShapeDtypeStruct((bn, s, d), f32),
                   jax.ShapeDtypeStruct((bn, s, ROUTE_LANES), f32)],
        compiler_params=_cparams(("parallel", "parallel")),
        name="merge",
    )(y_r, y_a, gates, x, g1.reshape(bn, 1, d), sh2.reshape(bn, 1, d), sc2.reshape(bn, 1, d),
      norm2_g.reshape(1, d), w_br, w_ba, w_out, w_rt, b_rt)


def _row_copy(src_hbm, row, dst, slot, sem):
    return pltpu.make_async_copy(src_hbm.at[pl.ds(row, 1)], dst.at[pl.ds(slot, 1)], sem)


def _moe_kernel(be_ref, src_ref, h2_hbm, wg_ref, wu_ref, wd_ref, o_ref, xbuf, sem):
    del be_ref
    base = pl.program_id(0) * MOE_ROWS

    def issue(r, carry):
        _row_copy(h2_hbm, src_ref[base + r], xbuf, r, sem).start()
        return carry

    def wait(r, carry):
        _row_copy(h2_hbm, 0, xbuf, r, sem).wait()
        return carry

    lax.fori_loop(0, MOE_ROWS, issue, 0, unroll=8)
    lax.fori_loop(0, MOE_ROWS, wait, 0, unroll=8)
    xb = xbuf[...].astype(bf16)
    gate = jnp.dot(xb, wg_ref[0], preferred_element_type=f32)
    up = jnp.dot(xb, wu_ref[0], preferred_element_type=f32)
    hid = gate * _sigmoid(gate) * up
    o_ref[...] = jnp.dot(hid.astype(bf16), wd_ref[0], preferred_element_type=f32)


def _moe(h2_flat, block_e, src_rows, wg, wu, wd):
    n_rows = src_rows.shape[0]
    d = h2_flat.shape[1]
    grid_spec = pltpu.PrefetchScalarGridSpec(
        num_scalar_prefetch=2,
        grid=(n_rows // MOE_ROWS,),
        in_specs=[pl.BlockSpec(memory_space=pl.ANY),
                  pl.BlockSpec((1, d, D_EXPERT), lambda i, be, src: (be[i], 0, 0)),
                  pl.BlockSpec((1, d, D_EXPERT), lambda i, be, src: (be[i], 0, 0)),
                  pl.BlockSpec((1, D_EXPERT, d), lambda i, be, src: (be[i], 0, 0))],
        out_specs=pl.BlockSpec((MOE_ROWS, d), lambda i, be, src: (i, 0)),
        scratch_shapes=[pltpu.VMEM((MOE_ROWS, d), f32), pltpu.SemaphoreType.DMA],
    )
    return pl.pallas_call(
        _moe_kernel,
        grid_spec=grid_spec,
        out_shape=jax.ShapeDtypeStruct((n_rows, d), f32),
        compiler_params=_cparams(("arbitrary",)),
        name="moe",
    )(block_e, src_rows, h2_flat, wg, wu, wd)


def _combine_kernel(d0_ref, d1_ref, yb_hbm, x1_ref, rt_ref, g2_ref, o_ref, buf0, buf1, sem):
    tc = buf0.shape[0]
    base = (pl.program_id(0) * pl.num_programs(1) + pl.program_id(1)) * tc

    def issue(r, carry):
        _row_copy(yb_hbm, d0_ref[base + r], buf0, r, sem).start()
        _row_copy(yb_hbm, d1_ref[base + r], buf1, r, sem).start()
        return carry

    def wait(r, carry):
        _row_copy(yb_hbm, 0, buf0, r, sem).wait()
        _row_copy(yb_hbm, 0, buf1, r, sem).wait()
        return carry

    lax.fori_loop(0, tc, issue, 0, unroll=8)
    lax.fori_loop(0, tc, wait, 0, unroll=8)
    rt = rt_ref[0]
    moe = rt[:, 0:1] * buf0[...] + rt[:, 1:2] * buf1[...]
    o_ref[0] = x1_ref[0] + g2_ref[0] * moe


def _combine(dest0, dest1, yb, x1, route, g2):
    bn, s, d = x1.shape
    tc = min(COMBINE_ROWS, s)
    row = lambda b, i, d0, d1: (b, i, 0)
    grid_spec = pltpu.PrefetchScalarGridSpec(
        num_scalar_prefetch=2,
        grid=(bn, s // tc),
        in_specs=[pl.BlockSpec(memory_space=pl.ANY),
                  pl.BlockSpec((1, tc, d), row),
                  pl.BlockSpec((1, tc, ROUTE_LANES), row),
                  pl.BlockSpec((1, 1, d), lambda b, i, d0, d1: (b, 0, 0))],
        out_specs=pl.BlockSpec((1, tc, d), row),
        scratch_shapes=[pltpu.VMEM((tc, d), f32), pltpu.VMEM((tc, d), f32),
                        pltpu.SemaphoreType.DMA],
    )
    return pl.pallas_call(
        _combine_kernel,
        grid_spec=grid_spec,
        out_shape=jax.ShapeDtypeStruct((bn, s, d), f32),
        compiler_params=_cparams(("arbitrary", "arbitrary")),
        name="combine",
    )(dest0, dest1, yb, x1, route, g2.reshape(bn, 1, d))


def _dispatch_plan(eid):
    n_tok = eid.shape[0]
    flat = eid.reshape(-1)
    n_asg = flat.shape[0]
    order = jnp.argsort(flat, stable=True)
    e_sorted = flat[order]
    counts = jnp.bincount(flat, length=N_EXPERTS)
    padded = (counts + MOE_ROWS - 1) // MOE_ROWS * MOE_ROWS
    starts = jnp.cumsum(counts) - counts
    pad_ends = jnp.cumsum(padded)
    pad_starts = pad_ends - padded
    dest_sorted = pad_starts[e_sorted] + jnp.arange(n_asg) - starts[e_sorted]
    n_rows = n_asg + N_EXPERTS * MOE_ROWS
    n_blocks = n_rows // MOE_ROWS
    src_rows = jnp.zeros((n_rows,), jnp.int32).at[dest_sorted].set((order // 2).astype(jnp.int32))
    dest = jnp.zeros((n_asg,), jnp.int32).at[order].set(dest_sorted.astype(jnp.int32))
    block_e = jnp.minimum(
        jnp.searchsorted(pad_ends, jnp.arange(n_blocks) * MOE_ROWS, side='right'),
        N_EXPERTS - 1).astype(jnp.int32)
    dest = dest.reshape(n_tok, 2)
    return block_e, src_rows, dest[:, 0], dest[:, 1]


def kernel(x, c, w_ada, b_ada, norm1_g, w_in, rwkv_mu, rwkv_w0, rwkv_w_up, rwkv_a0, rwkv_a_up, rwkv_g_up, rwkv_k_k, rwkv_k_a, rwkv_r_k, rwkv_lnx_g, rwkv_lnx_b, attn_q_g, attn_k_g, attn_rel_bias, w_branch_rwkv, w_branch_attn, w_out, norm2_g, router_coarse_w, router_coarse_b, router_fine_w, router_fine_b, expert_w_gate, expert_w_up, expert_w_down):
    bn, s, d = x.shape
    depth = w_ada.shape[0]
    for l in range(depth):
        mod = _ada(c, w_ada[l], b_ada[l])
        sh1, sc1, g1, sh2, sc2, g2 = [mod[:, i * d:(i + 1) * d] for i in range(6)]

        p_rwkv, p_att, gates = _inproj(x, norm1_g[l], sh1, sc1, w_in[l].astype(bf16))
        y_r = _rwkv(p_rwkv, rwkv_mu[l], rwkv_w0[l], rwkv_w_up[l], rwkv_a0[l], rwkv_a_up[l],
                    rwkv_g_up[l], rwkv_k_k[l], rwkv_k_a[l], rwkv_r_k[l].reshape(-1),
                    rwkv_lnx_g[l], rwkv_lnx_b[l])
        y_a = _attn(p_att, attn_q_g[l], attn_k_g[l], attn_rel_bias[l])

        n_rt = N_GROUPS + N_EXPERTS
        w_rt = jnp.zeros((d, ROUTE_LANES), f32)
        w_rt = w_rt.at[:, :N_GROUPS].set(router_coarse_w[l]).at[:, N_GROUPS:n_rt].set(router_fine_w[l])
        b_rt = jnp.zeros((1, ROUTE_LANES), f32)
        b_rt = b_rt.at[0, :N_GROUPS].set(router_coarse_b[l]).at[0, N_GROUPS:n_rt].set(router_fine_b[l])
        x1, h2, route = _merge(y_r, y_a, gates, x, g1, sh2, sc2, norm2_g[l],
                               w_branch_rwkv[l].astype(bf16), w_branch_attn[l].astype(bf16),
                               w_out[l].astype(bf16), w_rt, b_rt)

        route = route.reshape(bn * s, ROUTE_LANES)
        eid = route[:, 2:4].astype(jnp.int32)
        block_e, src_rows, dest0, dest1 = _dispatch_plan(eid)
        yb = _moe(h2.reshape(bn * s, d), block_e, src_rows, expert_w_gate[l].astype(bf16),
                  expert_w_up[l].astype(bf16), expert_w_down[l].astype(bf16))
        x = _combine(dest0, dest1, yb, x1, route.reshape(bn, s, ROUTE_LANES), g2)
    return x
```

```python
import functools

import numpy as np
import jax
import jax.numpy as jnp
from jax import lax
from jax.experimental import pallas as pl
from jax.experimental.pallas import tpu as pltpu

f32 = jnp.float32
bf16 = jnp.bfloat16
HI = lax.Precision.HIGHEST

D_MODEL = 1024
HEAD_DIM = 64
N_HEADS = 8
WIDTH = N_HEADS * HEAD_DIM
DECAY_LORA = 64
ICLR_LORA = 64
GATE_LORA = 128
RWKV_COLS = 3 * WIDTH + DECAY_LORA + ICLR_LORA + GATE_LORA
ATT_COLS = 3 * WIDTH
GATE_COLS = 2 * D_MODEL
CHUNK = 64
PREV_CHUNKS = 8
REL_CLIP = 128
N_GROUPS = 4
EXPERTS_PER_GROUP = 8
N_EXPERTS = N_GROUPS * EXPERTS_PER_GROUP
D_EXPERT = 512
NORM_EPS = 1e-6
GN_EPS = 64e-5
NEG_INF = -1e30

INPROJ_ROWS = 256
RWKV_CHUNK = 64
ATT_QROWS = 128
ATT_WINDOW = PREV_CHUNKS * CHUNK + ATT_QROWS
MERGE_ROWS = 512
MOE_ROWS = 128
COMBINE_ROWS = 256
ROUTE_LANES = 128
VMEM_LIMIT = 56 * 1024 * 1024


def _cparams(sem):
    return pltpu.CompilerParams(dimension_semantics=sem, vmem_limit_bytes=VMEM_LIMIT)


def _sigmoid(x):
    return 1.0 / (1.0 + jnp.exp(-x))


def _ada_kernel(c_ref, w_ref, b_ref, o_ref):
    c = c_ref[...]
    s = c * _sigmoid(c)
    o_ref[...] = jnp.dot(s, w_ref[...], precision=HI, preferred_element_type=f32) + b_ref[...]


def _ada(c, w, b):
    bn, d = c.shape
    n = w.shape[1]
    tn = 1024
    return pl.pallas_call(
        _ada_kernel,
        grid=(n // tn,),
        in_specs=[pl.BlockSpec((bn, d), lambda j: (0, 0)),
                  pl.BlockSpec((d, tn), lambda j: (0, j)),
                  pl.BlockSpec((1, tn), lambda j: (0, j))],
        out_specs=pl.BlockSpec((bn, tn), lambda j: (0, j)),
        out_shape=jax.ShapeDtypeStruct((bn, n), f32),
        compiler_params=_cparams(("arbitrary",)),
        name="ada",
    )(c, w, b.reshape(1, n))


def _rms_mod(x, gain, shift, scale):
    ms = jnp.mean(x * x, axis=-1, keepdims=True)
    h = x * lax.rsqrt(ms + NORM_EPS) * gain
    return h * (1.0 + scale) + shift


def _inproj_kernel(x_ref, g_ref, sh_ref, sc_ref, w_ref, pr_ref, pa_ref, pg_ref):
    h = _rms_mod(x_ref[0], g_ref[...], sh_ref[0], sc_ref[0]).astype(bf16)
    c1, c2 = RWKV_COLS, RWKV_COLS + ATT_COLS
    pr_ref[0] = jnp.dot(h, w_ref[:, :c1], preferred_element_type=f32)
    pa_ref[0] = jnp.dot(h, w_ref[:, c1:c2], preferred_element_type=f32)
    pg_ref[0] = _sigmoid(jnp.dot(h, w_ref[:, c2:], preferred_element_type=f32))


def _inproj(x, gain, shift, scale, w_bf):
    bn, s, d = x.shape
    tm = min(INPROJ_ROWS, s)
    n = w_bf.shape[1]
    row = lambda b, i: (b, i, 0)
    per_b = lambda b, i: (b, 0, 0)
    return pl.pallas_call(
        _inproj_kernel,
        grid=(bn, s // tm),
        in_specs=[pl.BlockSpec((1, tm, d), row),
                  pl.BlockSpec((1, d), lambda b, i: (0, 0)),
                  pl.BlockSpec((1, 1, d), per_b),
                  pl.BlockSpec((1, 1, d), per_b),
                  pl.BlockSpec((d, n), lambda b, i: (0, 0))],
        out_specs=[pl.BlockSpec((1, tm, RWKV_COLS), row),
                   pl.BlockSpec((1, tm, ATT_COLS), row),
                   pl.BlockSpec((1, tm, GATE_COLS), row)],
        out_shape=[jax.ShapeDtypeStruct((bn, s, RWKV_COLS), f32),
                   jax.ShapeDtypeStruct((bn, s, ATT_COLS), f32),
                   jax.ShapeDtypeStruct((bn, s, GATE_COLS), f32)],
        compiler_params=_cparams(("parallel", "parallel")),
        name="inproj",
    )(x, gain.reshape(1, d), shift.reshape(bn, 1, d), scale.reshape(bn, 1, d), w_bf)


def _mm(a, b):
    return jnp.dot(a, b, precision=HI, preferred_element_type=f32)


def _bmm(a, b):
    return jnp.dot(a, b, preferred_element_type=f32)


def _bmm_nt(a, b):
    return lax.dot_general(a, b, (((1,), (1,)), ((), ())), preferred_element_type=f32)


def _bmm_tn(a, b):
    return lax.dot_general(a, b, (((0,), (0,)), ((), ())), preferred_element_type=f32)


def _rwkv_kernel(p_ref, mu_ref, w0_ref, wup_ref, a0_ref, aup_ref, gup_ref, kk_ref, ka_ref,
                 rk_ref, lg_ref, lb_ref, o_ref, last_ref, st_ref):
    C = RWKV_CHUNK
    c_idx = pl.program_id(1)

    @pl.when(c_idx == 0)
    def _():
        last_ref[...] = jnp.zeros_like(last_ref)
        st_ref[...] = jnp.zeros_like(st_ref)

    pc = p_ref[0]
    row_full = lax.broadcasted_iota(jnp.int32, pc.shape, 0)
    prev = jnp.where(row_full == 0, last_ref[...], pltpu.roll(pc, 1, axis=0))
    last_ref[...] = pc[C - 1:C, :]
    ps = pc + (prev - pc) * mu_ref[...]

    o1, o2, o3 = WIDTH, 2 * WIDTH, 3 * WIDTH
    o4, o5 = o3 + DECAY_LORA, o3 + DECAY_LORA + ICLR_LORA
    r, k, v = ps[:, :o1], ps[:, o1:o2], ps[:, o2:o3]
    wd, ad, gd = ps[:, o3:o4], ps[:, o4:o5], ps[:, o5:]

    wraw = w0_ref[...] + _mm(jnp.tanh(wd), wup_ref[...])
    z = -wraw
    softplus = jnp.maximum(z, 0.0) + jnp.log(1.0 + jnp.exp(-jnp.abs(z)))
    logdec = -jnp.exp(-softplus - 0.5)
    a = _sigmoid(a0_ref[...] + _mm(ad, aup_ref[...]))
    g = _mm(_sigmoid(gd), gup_ref[...])
    kkr = k * kk_ref[...]
    k2 = k * (1.0 + (a - 1.0) * ka_ref[...])
    rkk = r * k2 * rk_ref[...]

    ri = lax.broadcasted_iota(jnp.int32, (C, C), 0)
    ci = lax.broadcasted_iota(jnp.int32, (C, C), 1)
    incl = ri >= ci
    strict = ri > ci
    eye = (ri == ci).astype(f32)
    cum = _mm(incl.astype(f32), logdec)
    e_pos = jnp.exp(cum)
    e_prev = jnp.exp(cum - logdec)
    e_neg = jnp.exp(-cum)
    cum_end = cum[C - 1:C, :]
    e_end = jnp.exp(cum_end - cum)
    g_end = jnp.exp(cum_end)

    blk16 = (ri // 16) == (ci // 16)
    blk32 = (ri // 32) == (ci // 32)
    m_diag16 = strict & blk16
    m_off32 = strict & blk32 & jnp.logical_not(blk16)
    m_off64 = strict & jnp.logical_not(blk32)
    ri2 = lax.broadcasted_iota(jnp.int32, (C, 2 * C), 0)
    ci2 = lax.broadcasted_iota(jnp.int32, (C, 2 * C), 1) % C
    strict2 = ri2 > ci2
    incl2 = ri2 >= ci2
    zeros_hh = jnp.zeros((C, HEAD_DIM), bf16)

    H = range(N_HEADS)
    hs = [slice(h * HEAD_DIM, (h + 1) * HEAD_DIM) for h in H]
    at, rt, bt, kt, bhat, khat, vb = [], [], [], [], [], [], []
    for sl in hs:
        kkh = kkr[:, sl]
        nrm = jnp.sqrt(jnp.sum(kkh * kkh, axis=-1, keepdims=True))
        kkh = kkh / jnp.maximum(nrm, 1e-12)
        bh = kkh * a[:, sl]
        at.append((-kkh * e_prev[:, sl]).astype(bf16))
        rt.append(r[:, sl] * e_pos[:, sl])
        bt.append((bh * e_neg[:, sl]).astype(bf16))
        kt.append((k2[:, sl] * e_neg[:, sl]).astype(bf16))
        bhat.append((bh * e_end[:, sl]).astype(bf16))
        khat.append((k2[:, sl] * e_end[:, sl]).astype(bf16))
        vb.append(v[:, sl].astype(bf16))

    smat = [_bmm_nt(jnp.concatenate([at[h], rt[h].astype(bf16)], axis=0),
                    jnp.concatenate([bt[h], kt[h]], axis=0)) for h in H]
    n_aak = [jnp.where(strict2, smat[h][:C], 0.0) for h in H]
    arb_ark = [jnp.where(incl2, smat[h][C:], 0.0).astype(bf16) for h in H]
    nmat = [n_aak[h][:, :HEAD_DIM] for h in H]

    pw = [jnp.where(m_diag16, nmat[h], 0.0) for h in H]
    tinv = [eye + pw[h] for h in H]
    for _ in range(3):
        pwb = [pw[h].astype(bf16) for h in H]
        pw = [_bmm(pwb[h], pwb[h]) for h in H]
        tinv = [tinv[h] + _bmm(tinv[h].astype(bf16), pw[h].astype(bf16)) for h in H]
    for m_off in (m_off32, m_off64):
        tb = [tinv[h].astype(bf16) for h in H]
        mid = [_bmm(jnp.where(m_off, nmat[h], 0.0).astype(bf16), tb[h]).astype(bf16) for h in H]
        tinv = [tinv[h] + _bmm(tb[h], mid[h]) for h in H]
    tb = [tinv[h].astype(bf16) for h in H]

    aakv = [_bmm(n_aak[h].astype(bf16), jnp.concatenate([zeros_hh, vb[h]], axis=0)).astype(bf16)
            for h in H]
    m1b = [_bmm(tb[h], at[h]).astype(bf16) for h in H]
    u0b = [_bmm(tb[h], aakv[h]).astype(bf16) for h in H]
    uv = [jnp.concatenate([u0b[h], vb[h]], axis=0) for h in H]
    m2 = [rt[h] + _bmm(arb_ark[h][:, :HEAD_DIM], m1b[h]) for h in H]
    y1 = [_bmm(arb_ark[h], uv[h]) for h in H]
    gmat = [eye * g_end[:, hs[h]] + _bmm_tn(bhat[h], m1b[h]) for h in H]
    dmat = [_bmm_tn(jnp.concatenate([bhat[h], khat[h]], axis=0), uv[h]) for h in H]
    upd = [_bmm(jnp.concatenate([m2[h], gmat[h]], axis=0).astype(bf16), st_ref[h].astype(bf16))
           for h in H]

    outs = []
    for h in H:
        sl = hs[h]
        st_ref[h] = upd[h][C:] + dmat[h]
        y = upd[h][:C] + y1[h]
        mean = jnp.mean(y, axis=-1, keepdims=True)
        yc = y - mean
        var = jnp.mean(yc * yc, axis=-1, keepdims=True)
        yn = yc * lax.rsqrt(var + GN_EPS) * lg_ref[:, sl] + lb_ref[:, sl]
        bonus = jnp.sum(rkk[:, sl], axis=-1, keepdims=True) * v[:, sl]
        outs.append((yn + bonus) * g[:, sl])
    o_ref[0] = jnp.concatenate(outs, axis=-1)


def _rwkv(p_rwkv, mu, w0, w_up, a0, a_up, g_up, k_k, k_a, r_k, lnx_g, lnx_b):
    bn, s, _ = p_rwkv.shape
    C = RWKV_CHUNK
    vec = lambda t: t.reshape(1, -1)
    const = lambda b, c: (0, 0)
    full = lambda t: pl.BlockSpec(t.shape, const)
    args = [vec(mu), vec(w0), w_up, vec(a0), a_up, g_up, vec(k_k), vec(k_a), vec(r_k),
            vec(lnx_g), vec(lnx_b)]
    return pl.pallas_call(
        _rwkv_kernel,
        grid=(bn, s // C),
        in_specs=[pl.BlockSpec((1, C, RWKV_COLS), lambda b, c: (b, c, 0))] + [full(t) for t in args],
        out_specs=pl.BlockSpec((1, C, WIDTH), lambda b, c: (b, c, 0)),
        out_shape=jax.ShapeDtypeStruct((bn, s, WIDTH), f32),
        scratch_shapes=[pltpu.VMEM((1, RWKV_COLS), f32),
                        pltpu.VMEM((N_HEADS, HEAD_DIM, HEAD_DIM), f32)],
        compiler_params=_cparams(("parallel", "arbitrary")),
        name="rwkv",
    )(p_rwkv, *args)


def _head_ms(x, ones_bd):
    return jnp.dot(x * x, ones_bd, precision=HI, preferred_element_type=f32) * (1.0 / HEAD_DIM)


def _attn_kernel(q_ref, k_ref, v_ref, bias_ref, qg_ref, kg_ref, o_ref, kpad_ref, vpad_ref):
    TQ = ATT_QROWS
    PAD = PREV_CHUNKS * CHUNK
    t = pl.program_id(2)
    r2 = lax.broadcasted_iota(jnp.int32, (2 * HEAD_DIM, 2 * HEAD_DIM), 0) // HEAD_DIM
    c2 = lax.broadcasted_iota(jnp.int32, (2 * HEAD_DIM, 2 * HEAD_DIM), 1) // HEAD_DIM
    ones_bd = (r2 == c2).astype(f32)

    @pl.when(t == 0)
    def _():
        kf = k_ref[0]
        kn = kf * lax.rsqrt(_head_ms(kf, ones_bd) + NORM_EPS) * kg_ref[...]
        kpad_ref[:PAD, :] = jnp.zeros((PAD, 2 * HEAD_DIM), bf16)
        vpad_ref[:PAD, :] = jnp.zeros((PAD, 2 * HEAD_DIM), bf16)
        kpad_ref[PAD:, :] = kn.astype(bf16)
        vpad_ref[PAD:, :] = v_ref[0].astype(bf16)

    qf = q_ref[0]
    qn = qf * lax.rsqrt(_head_ms(qf, ones_bd) + NORM_EPS) * (qg_ref[...] * HEAD_DIM ** -0.5)
    qn = qn.astype(bf16)
    start = pl.multiple_of(t * TQ, TQ)
    kw = kpad_ref[pl.ds(start, ATT_WINDOW), :]
    vw = vpad_ref[pl.ds(start, ATT_WINDOW), :]
    kpos = lax.broadcasted_iota(jnp.int32, (TQ, ATT_WINDOW), 1) + (t * TQ - PAD)
    outs = []
    for hh in range(2):
        sl = slice(hh * HEAD_DIM, (hh + 1) * HEAD_DIM)
        s = lax.dot_general(qn[:, sl], kw[:, sl], (((1,), (1,)), ((), ())),
                            preferred_element_type=f32)
        s = jnp.where(kpos >= 0, s + bias_ref[hh], NEG_INF)
        m = jnp.max(s, axis=-1, keepdims=True)
        e = jnp.exp(s - m)
        l = jnp.sum(e, axis=-1, keepdims=True)
        o = jnp.dot(e.astype(bf16), vw[:, sl], preferred_element_type=f32)
        outs.append(o / l)
    o_ref[0] = jnp.concatenate(outs, axis=-1)


def _attn_bias_table(rel_bias):
    pad = PREV_CHUNKS * CHUNK
    qi = np.arange(ATT_QROWS)[:, None]
    koff = np.arange(ATT_WINDOW)[None, :] - pad
    rel_idx = np.clip(koff - qi, -REL_CLIP, REL_CLIP) + REL_CLIP
    cq = qi // CHUNK
    kc = np.floor_divide(koff, CHUNK)
    visible = (kc >= cq - PREV_CHUNKS) & (kc <= cq)
    bias = rel_bias.astype(f32)[:, rel_idx]
    return jnp.where(jnp.asarray(visible)[None], bias, NEG_INF)


def _attn(p_att, q_g, k_g, rel_bias):
    bn, s, _ = p_att.shape
    TQ = ATT_QROWS
    hp = N_HEADS // 2
    pad = PREV_CHUNKS * CHUNK
    bias = _attn_bias_table(rel_bias)
    g2 = lambda t: jnp.tile(t.reshape(1, HEAD_DIM), (1, 2))
    return pl.pallas_call(
        _attn_kernel,
        grid=(bn, hp, s // TQ),
        in_specs=[pl.BlockSpec((1, TQ, 2 * HEAD_DIM), lambda b, j, t: (b, t, j)),
                  pl.BlockSpec((1, s, 2 * HEAD_DIM), lambda b, j, t: (b, 0, hp + j)),
                  pl.BlockSpec((1, s, 2 * HEAD_DIM), lambda b, j, t: (b, 0, 2 * hp + j)),
                  pl.BlockSpec((2, TQ, ATT_WINDOW), lambda b, j, t: (j, 0, 0)),
                  pl.BlockSpec((1, 2 * HEAD_DIM), lambda b, j, t: (0, 0)),
                  pl.BlockSpec((1, 2 * HEAD_DIM), lambda b, j, t: (0, 0))],
        out_specs=pl.BlockSpec((1, TQ, 2 * HEAD_DIM), lambda b, j, t: (b, t, j)),
        out_shape=jax.ShapeDtypeStruct((bn, s, WIDTH), f32),
        scratch_shapes=[pltpu.VMEM((pad + s, 2 * HEAD_DIM), bf16),
                        pltpu.VMEM((pad + s, 2 * HEAD_DIM), bf16)],
        compiler_params=_cparams(("parallel", "parallel", "arbitrary")),
        name="attn",
    )(p_att, p_att, p_att, bias, g2(q_g), g2(k_g))


def _route(logits):
    lane = lax.broadcasted_iota(jnp.int32, logits.shape, 1)
    big = jnp.int32(ROUTE_LANES)
    is_c = lane < N_GROUPS
    cl = jnp.where(is_c, logits, NEG_INF)
    cm = jnp.max(cl, axis=-1, keepdims=True)
    grp = jnp.min(jnp.where(cl == cm, lane, big), axis=-1, keepdims=True)
    p_grp = 1.0 / jnp.sum(jnp.where(is_c, jnp.exp(cl - cm), 0.0), axis=-1, keepdims=True)
    lo = N_GROUPS + grp * EXPERTS_PER_GROUP
    in_g = (lane >= lo) & (lane < lo + EXPERTS_PER_GROUP)
    fl = jnp.where(in_g, logits, NEG_INF)
    f1 = jnp.max(fl, axis=-1, keepdims=True)
    i1 = jnp.min(jnp.where(in_g & (fl == f1), lane, big), axis=-1, keepdims=True)
    fl2 = jnp.where(lane == i1, NEG_INF, fl)
    in_g2 = in_g & (lane != i1)
    f2 = jnp.max(fl2, axis=-1, keepdims=True)
    i2 = jnp.min(jnp.where(in_g2 & (fl2 == f2), lane, big), axis=-1, keepdims=True)
    e2 = jnp.exp(f2 - f1)
    w1 = p_grp / (1.0 + e2)
    w2 = p_grp * e2 / (1.0 + e2)
    out = jnp.where(lane == 0, w1, 0.0)
    out = jnp.where(lane == 1, w2, out)
    out = jnp.where(lane == 2, (i1 - N_GROUPS).astype(f32), out)
    out = jnp.where(lane == 3, (i2 - N_GROUPS).astype(f32), out)
    return out


def _merge_kernel(yr_ref, ya_ref, gt_ref, x_ref, g1_ref, sh_ref, sc_ref, ng_ref, wbr_ref,
                  wba_ref, wout_ref, wrt_ref, brt_ref, x1_ref, h2_ref, route_ref):
    t1 = jnp.dot(yr_ref[0].astype(bf16), wbr_ref[...], preferred_element_type=f32)
    t2 = jnp.dot(ya_ref[0].astype(bf16), wba_ref[...], preferred_element_type=f32)
    gt = gt_ref[0]
    m = gt[:, :D_MODEL] * t1 + gt[:, D_MODEL:] * t2
    mixed = jnp.dot(m.astype(bf16), wout_ref[...], preferred_element_type=f32)
    x1 = x_ref[0] + g1_ref[0] * mixed
    x1_ref[0] = x1
    h2 = _rms_mod(x1, ng_ref[...], sh_ref[0], sc_ref[0])
    h2_ref[0] = h2
    logits = jnp.dot(h2, wrt_ref[...], precision=HI, preferred_element_type=f32) + brt_ref[...]
    route_ref[0] = _route(logits)


def _merge(y_r, y_a, gates, x, g1, sh2, sc2, norm2_g, w_br, w_ba, w_out, w_rt, b_rt):
    bn, s, d = x.shape
    tm = min(MERGE_ROWS, s)
    row = lambda b, i: (b, i, 0)
    per_b = lambda b, i: (b, 0, 0)
    const = lambda b, i: (0, 0)
    full = lambda t: pl.BlockSpec(t.shape, const)
    return pl.pallas_call(
        _merge_kernel,
        grid=(bn, s // tm),
        in_specs=[pl.BlockSpec((1, tm, WIDTH), row),
                  pl.BlockSpec((1, tm, WIDTH), row),
                  pl.BlockSpec((1, tm, GATE_COLS), row),
                  pl.BlockSpec((1, tm, d), row),
                  pl.BlockSpec((1, 1, d), per_b),
                  pl.BlockSpec((1, 1, d), per_b),
                  pl.BlockSpec((1, 1, d), per_b),
                  pl.BlockSpec((1, d), const),
                  full(w_br), full(w_ba), full(w_out), full(w_rt), full(b_rt)],
        out_specs=[pl.BlockSpec((1, tm, d), row),
                   pl.BlockSpec((1, tm, d), row),
                   pl.BlockSpec((1, tm, ROUTE_LANES), row)],
        out_shape=[jax.ShapeDtypeStruct((bn, s, d), f32),
                   jax.ShapeDtypeStruct((bn, s, d), f32),
                   jax.ShapeDtypeStruct((bn, s, ROUTE_LANES), f32)],
        compiler_params=_cparams(("parallel", "parallel")),
        name="merge",
    )(y_r, y_a, gates, x, g1.reshape(bn, 1, d), sh2.reshape(bn, 1, d), sc2.reshape(bn, 1, d),
      norm2_g.reshape(1, d), w_br, w_ba, w_out, w_rt, b_rt)


def _row_copy(src_hbm, row, dst, slot, sem):
    return pltpu.make_async_copy(src_hbm.at[pl.ds(row, 1)], dst.at[pl.ds(slot, 1)], sem)


def _moe_kernel(be_ref, src_ref, h2_hbm, wg_ref, wu_ref, wd_ref, o_ref, xbuf, sem):
    del be_ref
    base = pl.program_id(0) * MOE_ROWS

    def issue(r, carry):
        _row_copy(h2_hbm, src_ref[base + r], xbuf, r, sem).start()
        return carry

    def wait(r, carry):
        _row_copy(h2_hbm, 0, xbuf, r, sem).wait()
        return carry

    lax.fori_loop(0, MOE_ROWS, issue, 0, unroll=8)
    lax.fori_loop(0, MOE_ROWS, wait, 0, unroll=8)
    xb = xbuf[...].astype(bf16)
    gate = jnp.dot(xb, wg_ref[0], preferred_element_type=f32)
    up = jnp.dot(xb, wu_ref[0], preferred_element_type=f32)
    hid = gate * _sigmoid(gate) * up
    o_ref[...] = jnp.dot(hid.astype(bf16), wd_ref[0], preferred_element_type=f32)


def _moe(h2_flat, block_e, src_rows, wg, wu, wd):
    n_rows = src_rows.shape[0]
    d = h2_flat.shape[1]
    grid_spec = pltpu.PrefetchScalarGridSpec(
        num_scalar_prefetch=2,
        grid=(n_rows // MOE_ROWS,),
        in_specs=[pl.BlockSpec(memory_space=pl.ANY),
                  pl.BlockSpec((1, d, D_EXPERT), lambda i, be, src: (be[i], 0, 0)),
                  pl.BlockSpec((1, d, D_EXPERT), lambda i, be, src: (be[i], 0, 0)),
                  pl.BlockSpec((1, D_EXPERT, d), lambda i, be, src: (be[i], 0, 0))],
        out_specs=pl.BlockSpec((MOE_ROWS, d), lambda i, be, src: (i, 0)),
        scratch_shapes=[pltpu.VMEM((MOE_ROWS, d), f32), pltpu.SemaphoreType.DMA],
    )
    return pl.pallas_call(
        _moe_kernel,
        grid_spec=grid_spec,
        out_shape=jax.ShapeDtypeStruct((n_rows, d), f32),
        compiler_params=_cparams(("arbitrary",)),
        name="moe",
    )(block_e, src_rows, h2_flat, wg, wu, wd)


def _combine_kernel(d0_ref, d1_ref, yb_hbm, x1_ref, rt_ref, g2_ref, o_ref, buf0, buf1, sem):
    tc = buf0.shape[0]
    base = (pl.program_id(0) * pl.num_programs(1) + pl.program_id(1)) * tc

    def issue(r, carry):
        _row_copy(yb_hbm, d0_ref[base + r], buf0, r, sem).start()
        _row_copy(yb_hbm, d1_ref[base + r], buf1, r, sem).start()
        return carry

    def wait(r, carry):
        _row_copy(yb_hbm, 0, buf0, r, sem).wait()
        _row_copy(yb_hbm, 0, buf1, r, sem).wait()
        return carry

    lax.fori_loop(0, tc, issue, 0, unroll=8)
    lax.fori_loop(0, tc, wait, 0, unroll=8)
    rt = rt_ref[0]
    moe = rt[:, 0:1] * buf0[...] + rt[:, 1:2] * buf1[...]
    o_ref[0] = x1_ref[0] + g2_ref[0] * moe


def _combine(dest0, dest1, yb, x1, route, g2):
    bn, s, d = x1.shape
    tc = min(COMBINE_ROWS, s)
    row = lambda b, i, d0, d1: (b, i, 0)
    grid_spec = pltpu.PrefetchScalarGridSpec(
        num_scalar_prefetch=2,
        grid=(bn, s // tc),
        in_specs=[pl.BlockSpec(memory_space=pl.ANY),
                  pl.BlockSpec((1, tc, d), row),
                  pl.BlockSpec((1, tc, ROUTE_LANES), row),
                  pl.BlockSpec((1, 1, d), lambda b, i, d0, d1: (b, 0, 0))],
        out_specs=pl.BlockSpec((1, tc, d), row),
        scratch_shapes=[pltpu.VMEM((tc, d), f32), pltpu.VMEM((tc, d), f32),
                        pltpu.SemaphoreType.DMA],
    )
    return pl.pallas_call(
        _combine_kernel,
        grid_spec=grid_spec,
        out_shape=jax.ShapeDtypeStruct((bn, s, d), f32),
        compiler_params=_cparams(("arbitrary", "arbitrary")),
        name="combine",
    )(dest0, dest1, yb, x1, route, g2.reshape(bn, 1, d))


def _dispatch_plan(eid):
    n_tok = eid.shape[0]
    flat = eid.reshape(-1)
    n_asg = flat.shape[0]
    order = jnp.argsort(flat, stable=True)
    e_sorted = flat[order]
    counts = jnp.bincount(flat, length=N_EXPERTS)
    padded = (counts + MOE_ROWS - 1) // MOE_ROWS * MOE_ROWS
    starts = jnp.cumsum(counts) - counts
    pad_ends = jnp.cumsum(padded)
    pad_starts = pad_ends - padded
    dest_sorted = pad_starts[e_sorted] + jnp.arange(n_asg) - starts[e_sorted]
    n_rows = n_asg + N_EXPERTS * MOE_ROWS
    n_blocks = n_rows // MOE_ROWS
    src_rows = jnp.zeros((n_rows,), jnp.int32).at[dest_sorted].set((order // 2).astype(jnp.int32))
    dest = jnp.zeros((n_asg,), jnp.int32).at[order].set(dest_sorted.astype(jnp.int32))
    block_e = jnp.minimum(
        jnp.searchsorted(pad_ends, jnp.arange(n_blocks) * MOE_ROWS, side='right'),
        N_EXPERTS - 1).astype(jnp.int32)
    dest = dest.reshape(n_tok, 2)
    return block_e, src_rows, dest[:, 0], dest[:, 1]


def kernel(x, c, w_ada, b_ada, norm1_g, w_in, rwkv_mu, rwkv_w0, rwkv_w_up, rwkv_a0, rwkv_a_up, rwkv_g_up, rwkv_k_k, rwkv_k_a, rwkv_r_k, rwkv_lnx_g, rwkv_lnx_b, attn_q_g, attn_k_g, attn_rel_bias, w_branch_rwkv, w_branch_attn, w_out, norm2_g, router_coarse_w, router_coarse_b, router_fine_w, router_fine_b, expert_w_gate, expert_w_up, expert_w_down):
    bn, s, d = x.shape
    depth = w_ada.shape[0]
    for l in range(depth):
        mod = _ada(c, w_ada[l], b_ada[l])
        sh1, sc1, g1, sh2, sc2, g2 = [mod[:, i * d:(i + 1) * d] for i in range(6)]

        p_rwkv, p_att, gates = _inproj(x, norm1_g[l], sh1, sc1, w_in[l].astype(bf16))
        y_r = _rwkv(p_rwkv, rwkv_mu[l], rwkv_w0[l], rwkv_w_up[l], rwkv_a0[l], rwkv_a_up[l],
                    rwkv_g_up[l], rwkv_k_k[l], rwkv_k_a[l], rwkv_r_k[l].reshape(-1),
                    rwkv_lnx_g[l], rwkv_lnx_b[l])
        y_a = _attn(p_att, attn_q_g[l], attn_k_g[l], attn_rel_bias[l])

        n_rt = N_GROUPS + N_EXPERTS
        w_rt = jnp.zeros((d, ROUTE_LANES), f32)
        w_rt = w_rt.at[:, :N_GROUPS].set(router_coarse_w[l]).at[:, N_GROUPS:n_rt].set(router_fine_w[l])
        b_rt = jnp.zeros((1, ROUTE_LANES), f32)
        b_rt = b_rt.at[0, :N_GROUPS].set(router_coarse_b[l]).at[0, N_GROUPS:n_rt].set(router_fine_b[l])
        x1, h2, route = _merge(y_r, y_a, gates, x, g1, sh2, sc2, norm2_g[l],
                               w_branch_rwkv[l].astype(bf16), w_branch_attn[l].astype(bf16),
                               w_out[l].astype(bf16), w_rt, b_rt)

        route = route.reshape(bn * s, ROUTE_LANES)
        eid = route[:, 2:4].astype(jnp.int32)
        block_e, src_rows, dest0, dest1 = _dispatch_plan(eid)
        yb = _moe(h2.reshape(bn * s, d), block_e, src_rows, expert_w_gate[l].astype(bf16),
                  expert_w_up[l].astype(bf16), expert_w_down[l].astype(bf16))
        x = _combine(dest0, dest1, yb, x1, route.reshape(bn, s, ROUTE_LANES), g2)
    return x
```

```python
import functools

import numpy as np
import jax
import jax.numpy as jnp
from jax import lax
from jax.experimental import pallas as pl
from jax.experimental.pallas import tpu as pltpu

f32 = jnp.float32
bf16 = jnp.bfloat16
HI = lax.Precision.HIGHEST

D_MODEL = 1024
HEAD_DIM = 64
N_HEADS = 8
WIDTH = N_HEADS * HEAD_DIM
DECAY_LORA = 64
ICLR_LORA = 64
GATE_LORA = 128
RWKV_COLS = 3 * WIDTH + DECAY_LORA + ICLR_LORA + GATE_LORA
ATT_COLS = 3 * WIDTH
GATE_COLS = 2 * D_MODEL
CHUNK = 64
PREV_CHUNKS = 8
REL_CLIP = 128
N_GROUPS = 4
EXPERTS_PER_GROUP = 8
N_EXPERTS = N_GROUPS * EXPERTS_PER_GROUP
D_EXPERT = 512
NORM_EPS = 1e-6
GN_EPS = 64e-5
NEG_INF = -1e30

INPROJ_ROWS = 256
RWKV_CHUNK = 64
ATT_QROWS = 128
ATT_WINDOW = PREV_CHUNKS * CHUNK + ATT_QROWS
MERGE_ROWS = 512
MOE_ROWS = 512
COMBINE_ROWS = 128
ROUTE_LANES = 128
ROW_TILES = D_MODEL // 128
VMEM_LIMIT = 56 * 1024 * 1024


def _cparams(sem):
    return pltpu.CompilerParams(dimension_semantics=sem, vmem_limit_bytes=VMEM_LIMIT)


def _sigmoid(x):
    return 1.0 / (1.0 + jnp.exp(-x))


def _ada_kernel(c_ref, w_ref, b_ref, o_ref):
    c = c_ref[...]
    s = c * _sigmoid(c)
    o_ref[...] = jnp.dot(s, w_ref[...], precision=HI, preferred_element_type=f32) + b_ref[...]


def _ada(c, w, b):
    bn, d = c.shape
    n = w.shape[1]
    tn = 1024
    return pl.pallas_call(
        _ada_kernel,
        grid=(n // tn,),
        in_specs=[pl.BlockSpec((bn, d), lambda j: (0, 0)),
                  pl.BlockSpec((d, tn), lambda j: (0, j)),
                  pl.BlockSpec((1, tn), lambda j: (0, j))],
        out_specs=pl.BlockSpec((bn, tn), lambda j: (0, j)),
        out_shape=jax.ShapeDtypeStruct((bn, n), f32),
        compiler_params=_cparams(("arbitrary",)),
        name="ada",
    )(c, w, b.reshape(1, n))


def _rms_mod(x, gain, shift, scale):
    ms = jnp.mean(x * x, axis=-1, keepdims=True)
    h = x * lax.rsqrt(ms + NORM_EPS) * gain
    return h * (1.0 + scale) + shift


def _inproj_kernel(x_ref, g_ref, sh_ref, sc_ref, w_ref, pr_ref, pa_ref, pg_ref):
    h = _rms_mod(x_ref[0], g_ref[...], sh_ref[0], sc_ref[0]).astype(bf16)
    c1, c2 = RWKV_COLS, RWKV_COLS + ATT_COLS
    pr_ref[0] = jnp.dot(h, w_ref[:, :c1], preferred_element_type=f32)
    pa_ref[0] = jnp.dot(h, w_ref[:, c1:c2], preferred_element_type=f32)
    pg_ref[0] = _sigmoid(jnp.dot(h, w_ref[:, c2:], preferred_element_type=f32))


def _inproj(x, gain, shift, scale, w_bf):
    bn, s, d = x.shape
    tm = min(INPROJ_ROWS, s)
    n = w_bf.shape[1]
    row = lambda b, i: (b, i, 0)
    per_b = lambda b, i: (b, 0, 0)
    return pl.pallas_call(
        _inproj_kernel,
        grid=(bn, s // tm),
        in_specs=[pl.BlockSpec((1, tm, d), row),
                  pl.BlockSpec((1, d), lambda b, i: (0, 0)),
                  pl.BlockSpec((1, 1, d), per_b),
                  pl.BlockSpec((1, 1, d), per_b),
                  pl.BlockSpec((d, n), lambda b, i: (0, 0))],
        out_specs=[pl.BlockSpec((1, tm, RWKV_COLS), row),
                   pl.BlockSpec((1, tm, ATT_COLS), row),
                   pl.BlockSpec((1, tm, GATE_COLS), row)],
        out_shape=[jax.ShapeDtypeStruct((bn, s, RWKV_COLS), f32),
                   jax.ShapeDtypeStruct((bn, s, ATT_COLS), f32),
                   jax.ShapeDtypeStruct((bn, s, GATE_COLS), f32)],
        compiler_params=_cparams(("parallel", "parallel")),
        name="inproj",
    )(x, gain.reshape(1, d), shift.reshape(bn, 1, d), scale.reshape(bn, 1, d), w_bf)


def _mm(a, b):
    return jnp.dot(a, b, precision=HI, preferred_element_type=f32)


def _bmm(a, b):
    return jnp.dot(a, b, preferred_element_type=f32)


def _bmm_nt(a, b):
    return lax.dot_general(a, b, (((1,), (1,)), ((), ())), preferred_element_type=f32)


def _bmm_tn(a, b):
    return lax.dot_general(a, b, (((0,), (0,)), ((), ())), preferred_element_type=f32)


def _rwkv_kernel(p_ref, mu_ref, w0_ref, wup_ref, a0_ref, aup_ref, gup_ref, kk_ref, ka_ref,
                 rk_ref, lg_ref, lb_ref, o_ref, last_ref, st_ref):
    C = RWKV_CHUNK
    c_idx = pl.program_id(1)

    @pl.when(c_idx == 0)
    def _():
        last_ref[...] = jnp.zeros_like(last_ref)
        st_ref[...] = jnp.zeros_like(st_ref)

    pc = p_ref[0]
    row_full = lax.broadcasted_iota(jnp.int32, pc.shape, 0)
    prev = jnp.where(row_full == 0, last_ref[...], pltpu.roll(pc, 1, axis=0))
    last_ref[...] = pc[C - 1:C, :]
    ps = pc + (prev - pc) * mu_ref[...]

    o1, o2, o3 = WIDTH, 2 * WIDTH, 3 * WIDTH
    o4, o5 = o3 + DECAY_LORA, o3 + DECAY_LORA + ICLR_LORA
    r, k, v = ps[:, :o1], ps[:, o1:o2], ps[:, o2:o3]
    wd, ad, gd = ps[:, o3:o4], ps[:, o4:o5], ps[:, o5:]

    wraw = w0_ref[...] + _mm(jnp.tanh(wd), wup_ref[...])
    z = -wraw
    softplus = jnp.maximum(z, 0.0) + jnp.log(1.0 + jnp.exp(-jnp.abs(z)))
    logdec = -jnp.exp(-softplus - 0.5)
    a = _sigmoid(a0_ref[...] + _mm(ad, aup_ref[...]))
    g = _mm(_sigmoid(gd), gup_ref[...])
    kkr = k * kk_ref[...]
    k2 = k * (1.0 + (a - 1.0) * ka_ref[...])
    rkk = r * k2 * rk_ref[...]

    ri = lax.broadcasted_iota(jnp.int32, (C, C), 0)
    ci = lax.broadcasted_iota(jnp.int32, (C, C), 1)
    incl = ri >= ci
    strict = ri > ci
    eye = (ri == ci).astype(f32)
    cum = _mm(incl.astype(f32), logdec)
    e_pos = jnp.exp(cum)
    e_prev = jnp.exp(cum - logdec)
    e_neg = jnp.exp(-cum)
    cum_end = cum[C - 1:C, :]
    e_end = jnp.exp(cum_end - cum)
    g_end = jnp.exp(cum_end)

    blk16 = (ri // 16) == (ci // 16)
    blk32 = (ri // 32) == (ci // 32)
    m_diag16 = strict & blk16
    m_off32 = strict & blk32 & jnp.logical_not(blk16)
    m_off64 = strict & jnp.logical_not(blk32)
    ri2 = lax.broadcasted_iota(jnp.int32, (C, 2 * C), 0)
    ci2 = lax.broadcasted_iota(jnp.int32, (C, 2 * C), 1) % C
    strict2 = ri2 > ci2
    incl2 = ri2 >= ci2
    zeros_hh = jnp.zeros((C, HEAD_DIM), bf16)

    H = range(N_HEADS)
    hs = [slice(h * HEAD_DIM, (h + 1) * HEAD_DIM) for h in H]
    at, rt, bt, kt, bhat, khat, vb = [], [], [], [], [], [], []
    for sl in hs:
        kkh = kkr[:, sl]
        nrm = jnp.sqrt(jnp.sum(kkh * kkh, axis=-1, keepdims=True))
        kkh = kkh / jnp.maximum(nrm, 1e-12)
        bh = kkh * a[:, sl]
        at.append((-kkh * e_prev[:, sl]).astype(bf16))
        rt.append(r[:, sl] * e_pos[:, sl])
        bt.append((bh * e_neg[:, sl]).astype(bf16))
        kt.append((k2[:, sl] * e_neg[:, sl]).astype(bf16))
        bhat.append((bh * e_end[:, sl]).astype(bf16))
        khat.append((k2[:, sl] * e_end[:, sl]).astype(bf16))
        vb.append(v[:, sl].astype(bf16))

    smat = [_bmm_nt(jnp.concatenate([at[h], rt[h].astype(bf16)], axis=0),
                    jnp.concatenate([bt[h], kt[h]], axis=0)) for h in H]
    n_aak = [jnp.where(strict2, smat[h][:C], 0.0) for h in H]
    arb_ark = [jnp.where(incl2, smat[h][C:], 0.0).astype(bf16) for h in H]
    nmat = [n_aak[h][:, :HEAD_DIM] for h in H]

    pw = [jnp.where(m_diag16, nmat[h], 0.0) for h in H]
    tinv = [eye + pw[h] for h in H]
    for _ in range(3):
        pwb = [pw[h].astype(bf16) for h in H]
        pw = [_bmm(pwb[h], pwb[h]) for h in H]
        tinv = [tinv[h] + _bmm(tinv[h].astype(bf16), pw[h].astype(bf16)) for h in H]
    for m_off in (m_off32, m_off64):
        tb = [tinv[h].astype(bf16) for h in H]
        mid = [_bmm(jnp.where(m_off, nmat[h], 0.0).astype(bf16), tb[h]).astype(bf16) for h in H]
        tinv = [tinv[h] + _bmm(tb[h], mid[h]) for h in H]
    tb = [tinv[h].astype(bf16) for h in H]

    aakv = [_bmm(n_aak[h].astype(bf16), jnp.concatenate([zeros_hh, vb[h]], axis=0)).astype(bf16)
            for h in H]
    m1b = [_bmm(tb[h], at[h]).astype(bf16) for h in H]
    u0b = [_bmm(tb[h], aakv[h]).astype(bf16) for h in H]
    uv = [jnp.concatenate([u0b[h], vb[h]], axis=0) for h in H]
    m2 = [rt[h] + _bmm(arb_ark[h][:, :HEAD_DIM], m1b[h]) for h in H]
    y1 = [_bmm(arb_ark[h], uv[h]) for h in H]
    gmat = [eye * g_end[:, hs[h]] + _bmm_tn(bhat[h], m1b[h]) for h in H]
    dmat = [_bmm_tn(jnp.concatenate([bhat[h], khat[h]], axis=0), uv[h]) for h in H]
    upd = [_bmm(jnp.concatenate([m2[h], gmat[h]], axis=0).astype(bf16), st_ref[h].astype(bf16))
           for h in H]

    outs = []
    for h in H:
        sl = hs[h]
        st_ref[h] = upd[h][C:] + dmat[h]
        y = upd[h][:C] + y1[h]
        mean = jnp.mean(y, axis=-1, keepdims=True)
        yc = y - mean
        var = jnp.mean(yc * yc, axis=-1, keepdims=True)
        yn = yc * lax.rsqrt(var + GN_EPS) * lg_ref[:, sl] + lb_ref[:, sl]
        bonus = jnp.sum(rkk[:, sl], axis=-1, keepdims=True) * v[:, sl]
        outs.append((yn + bonus) * g[:, sl])
    o_ref[0] = jnp.concatenate(outs, axis=-1)


def _rwkv(p_rwkv, mu, w0, w_up, a0, a_up, g_up, k_k, k_a, r_k, lnx_g, lnx_b):
    bn, s, _ = p_rwkv.shape
    C = RWKV_CHUNK
    vec = lambda t: t.reshape(1, -1)
    const = lambda b, c: (0, 0)
    full = lambda t: pl.BlockSpec(t.shape, const)
    args = [vec(mu), vec(w0), w_up, vec(a0), a_up, g_up, vec(k_k), vec(k_a), vec(r_k),
            vec(lnx_g), vec(lnx_b)]
    return pl.pallas_call(
        _rwkv_kernel,
        grid=(bn, s // C),
        in_specs=[pl.BlockSpec((1, C, RWKV_COLS), lambda b, c: (b, c, 0))] + [full(t) for t in args],
        out_specs=pl.BlockSpec((1, C, WIDTH), lambda b, c: (b, c, 0)),
        out_shape=jax.ShapeDtypeStruct((bn, s, WIDTH), f32),
        scratch_shapes=[pltpu.VMEM((1, RWKV_COLS), f32),
                        pltpu.VMEM((N_HEADS, HEAD_DIM, HEAD_DIM), f32)],
        compiler_params=_cparams(("parallel", "arbitrary")),
        name="rwkv",
    )(p_rwkv, *args)


def _head_ms(x, ones_bd):
    return jnp.dot(x * x, ones_bd, precision=HI, preferred_element_type=f32) * (1.0 / HEAD_DIM)


def _attn_kernel(q_ref, k_ref, v_ref, bias_ref, qg_ref, kg_ref, o_ref, kpad_ref, vpad_ref):
    TQ = ATT_QROWS
    PAD = PREV_CHUNKS * CHUNK
    t = pl.program_id(2)
    r2 = lax.broadcasted_iota(jnp.int32, (2 * HEAD_DIM, 2 * HEAD_DIM), 0) // HEAD_DIM
    c2 = lax.broadcasted_iota(jnp.int32, (2 * HEAD_DIM, 2 * HEAD_DIM), 1) // HEAD_DIM
    ones_bd = (r2 == c2).astype(f32)

    @pl.when(t == 0)
    def _():
        kf = k_ref[0]
        kn = kf * lax.rsqrt(_head_ms(kf, ones_bd) + NORM_EPS) * kg_ref[...]
        kpad_ref[:PAD, :] = jnp.zeros((PAD, 2 * HEAD_DIM), bf16)
        vpad_ref[:PAD, :] = jnp.zeros((PAD, 2 * HEAD_DIM), bf16)
        kpad_ref[PAD:, :] = kn.astype(bf16)
        vpad_ref[PAD:, :] = v_ref[0].astype(bf16)

    qf = q_ref[0]
    qn = qf * lax.rsqrt(_head_ms(qf, ones_bd) + NORM_EPS) * (qg_ref[...] * HEAD_DIM ** -0.5)
    qn = qn.astype(bf16)
    start = pl.multiple_of(t * TQ, TQ)
    kw = kpad_ref[pl.ds(start, ATT_WINDOW), :]
    vw = vpad_ref[pl.ds(start, ATT_WINDOW), :]
    kpos = lax.broadcasted_iota(jnp.int32, (TQ, ATT_WINDOW), 1) + (t * TQ - PAD)
    outs = []
    for hh in range(2):
        sl = slice(hh * HEAD_DIM, (hh + 1) * HEAD_DIM)
        s = lax.dot_general(qn[:, sl], kw[:, sl], (((1,), (1,)), ((), ())),
                            preferred_element_type=f32)
        s = jnp.where(kpos >= 0, s + bias_ref[hh], NEG_INF)
        m = jnp.max(s, axis=-1, keepdims=True)
        e = jnp.exp(s - m)
        l = jnp.sum(e, axis=-1, keepdims=True)
        o = jnp.dot(e.astype(bf16), vw[:, sl], preferred_element_type=f32)
        outs.append(o / l)
    o_ref[0] = jnp.concatenate(outs, axis=-1)


def _attn_bias_table(rel_bias):
    pad = PREV_CHUNKS * CHUNK
    qi = np.arange(ATT_QROWS)[:, None]
    koff = np.arange(ATT_WINDOW)[None, :] - pad
    rel_idx = np.clip(koff - qi, -REL_CLIP, REL_CLIP) + REL_CLIP
    cq = qi // CHUNK
    kc = np.floor_divide(koff, CHUNK)
    visible = (kc >= cq - PREV_CHUNKS) & (kc <= cq)
    bias = rel_bias.astype(f32)[:, rel_idx]
    return jnp.where(jnp.asarray(visible)[None], bias, NEG_INF)


def _attn(p_att, q_g, k_g, rel_bias):
    bn, s, _ = p_att.shape
    TQ = ATT_QROWS
    hp = N_HEADS // 2
    pad = PREV_CHUNKS * CHUNK
    bias = _attn_bias_table(rel_bias)
    g2 = lambda t: jnp.tile(t.reshape(1, HEAD_DIM), (1, 2))
    return pl.pallas_call(
        _attn_kernel,
        grid=(bn, hp, s // TQ),
        in_specs=[pl.BlockSpec((1, TQ, 2 * HEAD_DIM), lambda b, j, t: (b, t, j)),
                  pl.BlockSpec((1, s, 2 * HEAD_DIM), lambda b, j, t: (b, 0, hp + j)),
                  pl.BlockSpec((1, s, 2 * HEAD_DIM), lambda b, j, t: (b, 0, 2 * hp + j)),
                  pl.BlockSpec((2, TQ, ATT_WINDOW), lambda b, j, t: (j, 0, 0)),
                  pl.BlockSpec((1, 2 * HEAD_DIM), lambda b, j, t: (0, 0)),
                  pl.BlockSpec((1, 2 * HEAD_DIM), lambda b, j, t: (0, 0))],
        out_specs=pl.BlockSpec((1, TQ, 2 * HEAD_DIM), lambda b, j, t: (b, t, j)),
        out_shape=jax.ShapeDtypeStruct((bn, s, WIDTH), f32),
        scratch_shapes=[pltpu.VMEM((pad + s, 2 * HEAD_DIM), bf16),
                        pltpu.VMEM((pad + s, 2 * HEAD_DIM), bf16)],
        compiler_params=_cparams(("parallel", "parallel", "arbitrary")),
        name="attn",
    )(p_att, p_att, p_att, bias, g2(q_g), g2(k_g))


def _store_tile_rows(ref, x):
    rows = x.shape[0]
    for j in range(ROW_TILES):
        ref[pl.ds(j, rows, stride=ROW_TILES), :] = x[:, j * 128:(j + 1) * 128]


def _load_tile_rows(ref, rows):
    return jnp.concatenate(
        [ref[pl.ds(j, rows, stride=ROW_TILES), :] for j in range(ROW_TILES)], axis=1)


def _route(logits, tri, counts):
    lane = lax.broadcasted_iota(jnp.int32, logits.shape, 1)
    big = jnp.int32(ROUTE_LANES)
    is_c = lane < N_GROUPS
    cl = jnp.where(is_c, logits, NEG_INF)
    cm = jnp.max(cl, axis=-1, keepdims=True)
    grp = jnp.min(jnp.where(cl == cm, lane, big), axis=-1, keepdims=True)
    p_grp = 1.0 / jnp.sum(jnp.where(is_c, jnp.exp(cl - cm), 0.0), axis=-1, keepdims=True)
    lo = N_GROUPS + grp * EXPERTS_PER_GROUP
    in_g = (lane >= lo) & (lane < lo + EXPERTS_PER_GROUP)
    fl = jnp.where(in_g, logits, NEG_INF)
    f1 = jnp.max(fl, axis=-1, keepdims=True)
    i1 = jnp.min(jnp.where(in_g & (fl == f1), lane, big), axis=-1, keepdims=True)
    fl2 = jnp.where(lane == i1, NEG_INF, fl)
    in_g2 = in_g & (lane != i1)
    f2 = jnp.max(fl2, axis=-1, keepdims=True)
    i2 = jnp.min(jnp.where(in_g2 & (fl2 == f2), lane, big), axis=-1, keepdims=True)
    e2 = jnp.exp(f2 - f1)
    w1 = p_grp / (1.0 + e2)
    w2 = p_grp * e2 / (1.0 + e2)

    hot1 = (lane == i1).astype(f32)
    hot2 = (lane == i2).astype(f32)
    pre1 = jnp.dot(tri, hot1.astype(bf16), preferred_element_type=f32)
    pre2 = jnp.dot(tri, hot2.astype(bf16), preferred_element_type=f32)
    tot1 = jnp.sum(hot1, axis=0, keepdims=True)
    tot2 = jnp.sum(hot2, axis=0, keepdims=True)
    rank1 = jnp.sum(hot1 * (pre1 + counts), axis=-1, keepdims=True)
    rank2 = jnp.sum(hot2 * (pre2 + counts + tot1), axis=-1, keepdims=True)

    out = jnp.where(lane == 0, w1, 0.0)
    out = jnp.where(lane == 1, w2, out)
    out = jnp.where(lane == 2, (i1 - N_GROUPS).astype(f32), out)
    out = jnp.where(lane == 3, (i2 - N_GROUPS).astype(f32), out)
    out = jnp.where(lane == 4, rank1, out)
    out = jnp.where(lane == 5, rank2, out)
    return out, counts + tot1 + tot2


def _merge_kernel(yr_ref, ya_ref, gt_ref, x_ref, g1_ref, sh_ref, sc_ref, ng_ref, wbr_ref,
                  wba_ref, wout_ref, wrt_ref, brt_ref, tri_ref, x1_ref, h2_ref, route_ref,
                  cnt_ref):
    @pl.when((pl.program_id(0) == 0) & (pl.program_id(1) == 0))
    def _():
        cnt_ref[...] = jnp.zeros_like(cnt_ref)

    t1 = jnp.dot(yr_ref[0].astype(bf16), wbr_ref[...], preferred_element_type=f32)
    t2 = jnp.dot(ya_ref[0].astype(bf16), wba_ref[...], preferred_element_type=f32)
    gt = gt_ref[0]
    m = gt[:, :D_MODEL] * t1 + gt[:, D_MODEL:] * t2
    mixed = jnp.dot(m.astype(bf16), wout_ref[...], preferred_element_type=f32)
    x1 = x_ref[0] + g1_ref[0] * mixed
    x1_ref[0] = x1
    h2 = _rms_mod(x1, ng_ref[...], sh_ref[0], sc_ref[0])
    _store_tile_rows(h2_ref.at[0], h2)
    logits = jnp.dot(h2, wrt_ref[...], precision=HI, preferred_element_type=f32) + brt_ref[...]
    route, counts = _route(logits, tri_ref[...], cnt_ref[...])
    route_ref[0] = route
    cnt_ref[...] = counts


def _merge(y_r, y_a, gates, x, g1, sh2, sc2, norm2_g, w_br, w_ba, w_out, w_rt, b_rt):
    bn, s, d = x.shape
    tm = min(MERGE_ROWS, s)
    row = lambda b, i: (b, i, 0)
    per_b = lambda b, i: (b, 0, 0)
    const = lambda b, i: (0, 0)
    full = lambda t: pl.BlockSpec(t.shape, const)
    tri = jnp.asarray(np.tril(np.ones((tm, tm), np.float32), -1), bf16)
    return pl.pallas_call(
        _merge_kernel,
        grid=(bn, s // tm),
        in_specs=[pl.BlockSpec((1, tm, WIDTH), row),
                  pl.BlockSpec((1, tm, WIDTH), row),
                  pl.BlockSpec((1, tm, GATE_COLS), row),
                  pl.BlockSpec((1, tm, d), row),
                  pl.BlockSpec((1, 1, d), per_b),
                  pl.BlockSpec((1, 1, d), per_b),
                  pl.BlockSpec((1, 1, d), per_b),
                  pl.BlockSpec((1, d), const),
                  full(w_br), full(w_ba), full(w_out), full(w_rt), full(b_rt), full(tri)],
        out_specs=[pl.BlockSpec((1, tm, d), row),
                   pl.BlockSpec((1, tm * ROW_TILES, 128), row),
                   pl.BlockSpec((1, tm, ROUTE_LANES), row),
                   pl.BlockSpec((1, ROUTE_LANES), const)],
        out_shape=[jax.ShapeDtypeStruct((bn, s, d), f32),
                   jax.ShapeDtypeStruct((bn, s * ROW_TILES, 128), f32),
                   jax.ShapeDtypeStruct((bn, s, ROUTE_LANES), f32),
                   jax.ShapeDtypeStruct((1, ROUTE_LANES), f32)],
        compiler_params=_cparams(("arbitrary", "arbitrary")),
        name="merge",
    )(y_r, y_a, gates, x, g1.reshape(bn, 1, d), sh2.reshape(bn, 1, d), sc2.reshape(bn, 1, d),
      norm2_g.reshape(1, d), w_br, w_ba, w_out, w_rt, b_rt, tri)


def _row_copy(src, src_row, dst, dst_row, sem):
    return pltpu.make_async_copy(src.at[src_row], dst.at[pl.ds(dst_row * ROW_TILES, ROW_TILES)],
                                 sem)


def _moe_kernel(tok_ref, be_ref, j0_ref, h2_hbm, wg_ref, wu_ref, wd_ref, o_ref, xbuf, sem):
    del be_ref
    i = pl.program_id(0)
    last = pl.num_programs(0) - 1
    n_asg = tok_ref.shape[0]

    def issue(blk, slot):
        j0 = j0_ref[blk]
        for r in range(MOE_ROWS):
            tok = tok_ref[jnp.minimum(j0 + r, n_asg - 1)]
            _row_copy(h2_hbm, tok, xbuf.at[slot], r, sem.at[slot]).start()

    def wait(slot):
        for r in range(MOE_ROWS):
            _row_copy(h2_hbm, 0, xbuf.at[slot], r, sem.at[slot]).wait()

    for slot in range(2):
        @pl.when(i % 2 == slot)
        def _(slot=slot):
            if slot == 0:
                @pl.when(i == 0)
                def _():
                    issue(0, 0)

            wait(slot)
            issue(jnp.minimum(i + 1, last), 1 - slot)
            xb = _load_tile_rows(xbuf.at[slot], MOE_ROWS).astype(bf16)
            gate = jnp.dot(xb, wg_ref[0], preferred_element_type=f32)
            up = jnp.dot(xb, wu_ref[0], preferred_element_type=f32)
            hid = gate * _sigmoid(gate) * up
            _store_tile_rows(
                o_ref, jnp.dot(hid.astype(bf16), wd_ref[0], preferred_element_type=f32))

            @pl.when(i == last)
            def _():
                wait(1 - slot)


def _moe(h2_tiles, tok_sorted, block_e, block_j0, wg, wu, wd):
    d = D_MODEL
    n_blocks = block_e.shape[0]
    blk_rows = MOE_ROWS * ROW_TILES
    by_expert = lambda i, tok, be, j0: (be[i], 0, 0)
    grid_spec = pltpu.PrefetchScalarGridSpec(
        num_scalar_prefetch=3,
        grid=(n_blocks,),
        in_specs=[pl.BlockSpec(memory_space=pl.ANY),
                  pl.BlockSpec((1, d, D_EXPERT), by_expert),
                  pl.BlockSpec((1, d, D_EXPERT), by_expert),
                  pl.BlockSpec((1, D_EXPERT, d), by_expert)],
        out_specs=pl.BlockSpec((blk_rows, 128), lambda i, tok, be, j0: (i, 0)),
        scratch_shapes=[pltpu.VMEM((2, blk_rows, 128), f32), pltpu.SemaphoreType.DMA((2,))],
    )
    yb = pl.pallas_call(
        _moe_kernel,
        grid_spec=grid_spec,
        out_shape=jax.ShapeDtypeStruct((n_blocks * blk_rows, 128), f32),
        compiler_params=_cparams(("arbitrary",)),
        name="moe",
    )(tok_sorted, block_e, block_j0, h2_tiles, wg, wu, wd)
    return yb.reshape(n_blocks * MOE_ROWS, ROW_TILES, 128)


def _combine_kernel(d0_ref, d1_ref, yb_hbm, x1_ref, rt_ref, g2_ref, o_ref, buf, sem):
    tc = buf.shape[2] // ROW_TILES
    step = pl.program_id(0) * pl.num_programs(1) + pl.program_id(1)
    n_steps = pl.num_programs(0) * pl.num_programs(1)

    def issue(for_step, slot):
        base = for_step * tc
        for r in range(tc):
            _row_copy(yb_hbm, d0_ref[base + r], buf.at[slot, 0], r, sem.at[slot]).start()
            _row_copy(yb_hbm, d1_ref[base + r], buf.at[slot, 1], r, sem.at[slot]).start()

    def wait(slot):
        for r in range(tc):
            _row_copy(yb_hbm, 0, buf.at[slot, 0], r, sem.at[slot]).wait()
            _row_copy(yb_hbm, 0, buf.at[slot, 1], r, sem.at[slot]).wait()

    @pl.when(step == 0)
    def _():
        issue(step, 0)

    for slot in range(2):
        @pl.when(step % 2 == slot)
        def _(slot=slot):
            @pl.when(step + 1 < n_steps)
            def _():
                issue(step + 1, 1 - slot)

            wait(slot)
            rt = rt_ref[0]
            moe = (rt[:, 0:1] * _load_tile_rows(buf.at[slot, 0], tc)
                   + rt[:, 1:2] * _load_tile_rows(buf.at[slot, 1], tc))
            o_ref[0] = x1_ref[0] + g2_ref[0] * moe


def _combine(dest0, dest1, yb, x1, route, g2):
    bn, s, d = x1.shape
    tc = min(COMBINE_ROWS, s)
    row = lambda b, i, d0, d1: (b, i, 0)
    grid_spec = pltpu.PrefetchScalarGridSpec(
        num_scalar_prefetch=2,
        grid=(bn, s // tc),
        in_specs=[pl.BlockSpec(memory_space=pl.ANY),
                  pl.BlockSpec((1, tc, d), row),
                  pl.BlockSpec((1, tc, ROUTE_LANES), row),
                  pl.BlockSpec((1, 1, d), lambda b, i, d0, d1: (b, 0, 0))],
        out_specs=pl.BlockSpec((1, tc, d), row),
        scratch_shapes=[pltpu.VMEM((2, 2, tc * ROW_TILES, 128), f32),
                        pltpu.SemaphoreType.DMA((2,))],
    )
    return pl.pallas_call(
        _combine_kernel,
        grid_spec=grid_spec,
        out_shape=jax.ShapeDtypeStruct((bn, s, d), f32),
        compiler_params=_cparams(("arbitrary", "arbitrary")),
        name="combine",
    )(dest0, dest1, yb, x1, route, g2.reshape(bn, 1, d))


def _dispatch_plan(route, counts):
    n_tok = route.shape[0]
    counts = counts.astype(jnp.int32)
    padded = (counts + MOE_ROWS - 1) // MOE_ROWS * MOE_ROWS
    pad_ends = jnp.cumsum(padded)
    pad_starts = pad_ends - padded
    starts = jnp.cumsum(counts) - counts
    n_blocks = (2 * n_tok + N_EXPERTS * MOE_ROWS) // MOE_ROWS
    eid = route[:, 2:4].astype(jnp.int32)
    rank = route[:, 4:6].astype(jnp.int32)
    hot = eid[:, :, None] == jnp.arange(N_EXPERTS, dtype=jnp.int32)
    dest = rank + jnp.sum(jnp.where(hot, pad_starts, 0), axis=-1)
    tok = jnp.broadcast_to(jnp.arange(n_tok, dtype=jnp.int32)[:, None], (n_tok, 2))
    _, tok_sorted = lax.sort((dest.reshape(-1), tok.reshape(-1)), num_keys=1)
    block_row = jnp.arange(n_blocks, dtype=jnp.int32) * MOE_ROWS
    block_e = jnp.minimum(jnp.sum(pad_ends[None, :] <= block_row[:, None], axis=-1),
                          N_EXPERTS - 1).astype(jnp.int32)
    hot_b = block_e[:, None] == jnp.arange(N_EXPERTS, dtype=jnp.int32)
    block_j0 = block_row + jnp.sum(jnp.where(hot_b, starts - pad_starts, 0), axis=-1)
    block_j0 = jnp.clip(block_j0, 0, 2 * n_tok - 1).astype(jnp.int32)
    return tok_sorted, block_e, block_j0, dest[:, 0], dest[:, 1]


def kernel(x, c, w_ada, b_ada, norm1_g, w_in, rwkv_mu, rwkv_w0, rwkv_w_up, rwkv_a0, rwkv_a_up, rwkv_g_up, rwkv_k_k, rwkv_k_a, rwkv_r_k, rwkv_lnx_g, rwkv_lnx_b, attn_q_g, attn_k_g, attn_rel_bias, w_branch_rwkv, w_branch_attn, w_out, norm2_g, router_coarse_w, router_coarse_b, router_fine_w, router_fine_b, expert_w_gate, expert_w_up, expert_w_down):
    bn, s, d = x.shape
    depth = w_ada.shape[0]
    for l in range(depth):
        mod = _ada(c, w_ada[l], b_ada[l])
        sh1, sc1, g1, sh2, sc2, g2 = [mod[:, i * d:(i + 1) * d] for i in range(6)]

        p_rwkv, p_att, gates = _inproj(x, norm1_g[l], sh1, sc1, w_in[l].astype(bf16))
        y_r = _rwkv(p_rwkv, rwkv_mu[l], rwkv_w0[l], rwkv_w_up[l], rwkv_a0[l], rwkv_a_up[l],
                    rwkv_g_up[l], rwkv_k_k[l], rwkv_k_a[l], rwkv_r_k[l].reshape(-1),
                    rwkv_lnx_g[l], rwkv_lnx_b[l])
        y_a = _attn(p_att, attn_q_g[l], attn_k_g[l], attn_rel_bias[l])

        n_rt = N_GROUPS + N_EXPERTS
        w_rt = jnp.zeros((d, ROUTE_LANES), f32)
        w_rt = w_rt.at[:, :N_GROUPS].set(router_coarse_w[l]).at[:, N_GROUPS:n_rt].set(router_fine_w[l])
        b_rt = jnp.zeros((1, ROUTE_LANES), f32)
        b_rt = b_rt.at[0, :N_GROUPS].set(router_coarse_b[l]).at[0, N_GROUPS:n_rt].set(router_fine_b[l])
        x1, h2, route, counts = _merge(y_r, y_a, gates, x, g1, sh2, sc2, norm2_g[l],
                                       w_branch_rwkv[l].astype(bf16),
                                       w_branch_attn[l].astype(bf16),
                                       w_out[l].astype(bf16), w_rt, b_rt)

        tok_sorted, block_e, block_j0, dest0, dest1 = _dispatch_plan(
            route.reshape(bn * s, ROUTE_LANES), counts[0, N_GROUPS:n_rt])
        yb = _moe(h2.reshape(bn * s, ROW_TILES, 128), tok_sorted, block_e, block_j0,
                  expert_w_gate[l].astype(bf16), expert_w_up[l].astype(bf16),
                  expert_w_down[l].astype(bf16))
        x = _combine(dest0, dest1, yb, x1, route, g2)
    return x
```

```python
import functools

import numpy as np
import jax
import jax.numpy as jnp
from jax import lax
from jax.experimental import pallas as pl
from jax.experimental.pallas import tpu as pltpu

f32 = jnp.float32
bf16 = jnp.bfloat16
HI = lax.Precision.HIGHEST

D_MODEL = 1024
HEAD_DIM = 64
N_HEADS = 8
WIDTH = N_HEADS * HEAD_DIM
DECAY_LORA = 64
ICLR_LORA = 64
GATE_LORA = 128
RWKV_COLS = 3 * WIDTH + DECAY_LORA + ICLR_LORA + GATE_LORA
ATT_COLS = 3 * WIDTH
GATE_COLS = 2 * D_MODEL
CHUNK = 64
PREV_CHUNKS = 8
REL_CLIP = 128
N_GROUPS = 4
EXPERTS_PER_GROUP = 8
N_EXPERTS = N_GROUPS * EXPERTS_PER_GROUP
D_EXPERT = 512
NORM_EPS = 1e-6
GN_EPS = 64e-5
NEG_INF = -1e30

INPROJ_ROWS = 256
RWKV_CHUNK = 64
ATT_QROWS = 128
ATT_WINDOW = PREV_CHUNKS * CHUNK + ATT_QROWS
MERGE_ROWS = 512
MOE_ROWS = 512
COMBINE_ROWS = 128
ROUTE_LANES = 128
ROW_TILES = D_MODEL // 128
VMEM_LIMIT = 56 * 1024 * 1024


def _cparams(sem):
    return pltpu.CompilerParams(dimension_semantics=sem, vmem_limit_bytes=VMEM_LIMIT)


def _sigmoid(x):
    return 1.0 / (1.0 + jnp.exp(-x))


def _ada_kernel(c_ref, w_ref, b_ref, o_ref):
    c = c_ref[...]
    s = c * _sigmoid(c)
    o_ref[...] = jnp.dot(s, w_ref[...], precision=HI, preferred_element_type=f32) + b_ref[...]


def _ada(c, w, b):
    bn, d = c.shape
    n = w.shape[1]
    tn = 1024
    return pl.pallas_call(
        _ada_kernel,
        grid=(n // tn,),
        in_specs=[pl.BlockSpec((bn, d), lambda j: (0, 0)),
                  pl.BlockSpec((d, tn), lambda j: (0, j)),
                  pl.BlockSpec((1, tn), lambda j: (0, j))],
        out_specs=pl.BlockSpec((bn, tn), lambda j: (0, j)),
        out_shape=jax.ShapeDtypeStruct((bn, n), f32),
        compiler_params=_cparams(("arbitrary",)),
        name="ada",
    )(c, w, b.reshape(1, n))


def _rms_mod(x, gain, shift, scale):
    ms = jnp.mean(x * x, axis=-1, keepdims=True)
    h = x * lax.rsqrt(ms + NORM_EPS) * gain
    return h * (1.0 + scale) + shift


def _head_norm(x, gain, ones_bd):
    outs = []
    for c in range(WIDTH // 128):
        xc = x[:, c * 128:(c + 1) * 128]
        ms = jnp.dot((xc * xc).astype(bf16), ones_bd, preferred_element_type=f32) * (1.0 / HEAD_DIM)
        outs.append(xc * lax.rsqrt(ms + NORM_EPS))
    return jnp.concatenate(outs, axis=1) * gain


def _inproj_kernel(x_ref, g_ref, sh_ref, sc_ref, w_ref, qg_ref, kg_ref, bd_ref,
                   pr_ref, q_ref, k_ref, v_ref, pg_ref):
    h = _rms_mod(x_ref[0], g_ref[...], sh_ref[0], sc_ref[0]).astype(bf16)
    c1, c2 = RWKV_COLS, RWKV_COLS + ATT_COLS
    pr_ref[0] = jnp.dot(h, w_ref[:, :c1], preferred_element_type=f32)
    pa = jnp.dot(h, w_ref[:, c1:c2], preferred_element_type=f32)
    ones_bd = bd_ref[...]
    q_ref[0] = _head_norm(pa[:, :WIDTH], qg_ref[...], ones_bd).astype(bf16)
    k_ref[0] = _head_norm(pa[:, WIDTH:2 * WIDTH], kg_ref[...], ones_bd).astype(bf16)
    v_ref[0] = pa[:, 2 * WIDTH:].astype(bf16)
    pg_ref[0] = _sigmoid(jnp.dot(h, w_ref[:, c2:], preferred_element_type=f32))


def _inproj(x, gain, shift, scale, w_bf, q_g, k_g):
    bn, s, d = x.shape
    tm = min(INPROJ_ROWS, s)
    n = w_bf.shape[1]
    row = lambda b, i: (b, i, 0)
    per_b = lambda b, i: (b, 0, 0)
    const = lambda b, i: (0, 0)
    per_head = lambda g, scale: (jnp.tile(g.astype(f32), N_HEADS) * scale).reshape(1, WIDTH)
    lane_head = np.arange(128) // HEAD_DIM
    ones_bd = jnp.asarray(lane_head[:, None] == lane_head[None, :], bf16)
    return pl.pallas_call(
        _inproj_kernel,
        grid=(bn, s // tm),
        in_specs=[pl.BlockSpec((1, tm, d), row),
                  pl.BlockSpec((1, d), const),
                  pl.BlockSpec((1, 1, d), per_b),
                  pl.BlockSpec((1, 1, d), per_b),
                  pl.BlockSpec((d, n), const),
                  pl.BlockSpec((1, WIDTH), const),
                  pl.BlockSpec((1, WIDTH), const),
                  pl.BlockSpec((128, 128), const)],
        out_specs=[pl.BlockSpec((1, tm, RWKV_COLS), row),
                   pl.BlockSpec((1, tm, WIDTH), row),
                   pl.BlockSpec((1, tm, WIDTH), row),
                   pl.BlockSpec((1, tm, WIDTH), row),
                   pl.BlockSpec((1, tm, GATE_COLS), row)],
        out_shape=[jax.ShapeDtypeStruct((bn, s, RWKV_COLS), f32),
                   jax.ShapeDtypeStruct((bn, s, WIDTH), bf16),
                   jax.ShapeDtypeStruct((bn, s, WIDTH), bf16),
                   jax.ShapeDtypeStruct((bn, s, WIDTH), bf16),
                   jax.ShapeDtypeStruct((bn, s, GATE_COLS), f32)],
        compiler_params=_cparams(("parallel", "parallel")),
        name="inproj",
    )(x, gain.reshape(1, d), shift.reshape(bn, 1, d), scale.reshape(bn, 1, d), w_bf,
      per_head(q_g, HEAD_DIM ** -0.5), per_head(k_g, 1.0), ones_bd)


def _mm(a, b):
    return jnp.dot(a, b, precision=HI, preferred_element_type=f32)


def _bmm(a, b):
    return jnp.dot(a, b, preferred_element_type=f32)


def _bmm_nt(a, b):
    return lax.dot_general(a, b, (((1,), (1,)), ((), ())), preferred_element_type=f32)


def _bmm_tn(a, b):
    return lax.dot_general(a, b, (((0,), (0,)), ((), ())), preferred_element_type=f32)


def _rwkv_kernel(p_ref, mu_ref, w0_ref, wup_ref, a0_ref, aup_ref, gup_ref, kk_ref, ka_ref,
                 rk_ref, lg_ref, lb_ref, o_ref, last_ref, st_ref):
    C = RWKV_CHUNK
    c_idx = pl.program_id(1)

    @pl.when(c_idx == 0)
    def _():
        last_ref[...] = jnp.zeros_like(last_ref)
        st_ref[...] = jnp.zeros_like(st_ref)

    pc = p_ref[0]
    row_full = lax.broadcasted_iota(jnp.int32, pc.shape, 0)
    prev = jnp.where(row_full == 0, last_ref[...], pltpu.roll(pc, 1, axis=0))
    last_ref[...] = pc[C - 1:C, :]
    ps = pc + (prev - pc) * mu_ref[...]

    o1, o2, o3 = WIDTH, 2 * WIDTH, 3 * WIDTH
    o4, o5 = o3 + DECAY_LORA, o3 + DECAY_LORA + ICLR_LORA
    r, k, v = ps[:, :o1], ps[:, o1:o2], ps[:, o2:o3]
    wd, ad, gd = ps[:, o3:o4], ps[:, o4:o5], ps[:, o5:]

    wraw = w0_ref[...] + _mm(jnp.tanh(wd), wup_ref[...])
    z = -wraw
    softplus = jnp.maximum(z, 0.0) + jnp.log(1.0 + jnp.exp(-jnp.abs(z)))
    logdec = -jnp.exp(-softplus - 0.5)
    a = _sigmoid(a0_ref[...] + _mm(ad, aup_ref[...]))
    g = _mm(_sigmoid(gd), gup_ref[...])
    kkr = k * kk_ref[...]
    k2 = k * (1.0 + (a - 1.0) * ka_ref[...])
    rkk = r * k2 * rk_ref[...]

    ri = lax.broadcasted_iota(jnp.int32, (C, C), 0)
    ci = lax.broadcasted_iota(jnp.int32, (C, C), 1)
    incl = ri >= ci
    strict = ri > ci
    eye = (ri == ci).astype(f32)
    cum = _mm(incl.astype(f32), logdec)
    e_pos = jnp.exp(cum)
    e_prev = jnp.exp(cum - logdec)
    e_neg = jnp.exp(-cum)
    cum_end = cum[C - 1:C, :]
    e_end = jnp.exp(cum_end - cum)
    g_end = jnp.exp(cum_end)

    blk16 = (ri // 16) == (ci // 16)
    blk32 = (ri // 32) == (ci // 32)
    m_diag16 = strict & blk16
    m_off32 = strict & blk32 & jnp.logical_not(blk16)
    m_off64 = strict & jnp.logical_not(blk32)
    ri2 = lax.broadcasted_iota(jnp.int32, (C, 2 * C), 0)
    ci2 = lax.broadcasted_iota(jnp.int32, (C, 2 * C), 1) % C
    strict2 = ri2 > ci2
    incl2 = ri2 >= ci2
    zeros_hh = jnp.zeros((C, HEAD_DIM), bf16)

    H = range(N_HEADS)
    hs = [slice(h * HEAD_DIM, (h + 1) * HEAD_DIM) for h in H]
    at, rt, bt, kt, bhat, khat, vb = [], [], [], [], [], [], []
    for sl in hs:
        kkh = kkr[:, sl]
        nrm = jnp.sqrt(jnp.sum(kkh * kkh, axis=-1, keepdims=True))
        kkh = kkh / jnp.maximum(nrm, 1e-12)
        bh = kkh * a[:, sl]
        at.append((-kkh * e_prev[:, sl]).astype(bf16))
        rt.append(r[:, sl] * e_pos[:, sl])
        bt.append((bh * e_neg[:, sl]).astype(bf16))
        kt.append((k2[:, sl] * e_neg[:, sl]).astype(bf16))
        bhat.append((bh * e_end[:, sl]).astype(bf16))
        khat.append((k2[:, sl] * e_end[:, sl]).astype(bf16))
        vb.append(v[:, sl].astype(bf16))

    smat = [_bmm_nt(jnp.concatenate([at[h], rt[h].astype(bf16)], axis=0),
                    jnp.concatenate([bt[h], kt[h]], axis=0)) for h in H]
    n_aak = [jnp.where(strict2, smat[h][:C], 0.0) for h in H]
    arb_ark = [jnp.where(incl2, smat[h][C:], 0.0).astype(bf16) for h in H]
    nmat = [n_aak[h][:, :HEAD_DIM] for h in H]

    pw = [jnp.where(m_diag16, nmat[h], 0.0) for h in H]
    tinv = [eye + pw[h] for h in H]
    for _ in range(3):
        pwb = [pw[h].astype(bf16) for h in H]
        pw = [_bmm(pwb[h], pwb[h]) for h in H]
        tinv = [tinv[h] + _bmm(tinv[h].astype(bf16), pw[h].astype(bf16)) for h in H]
    for m_off in (m_off32, m_off64):
        tb = [tinv[h].astype(bf16) for h in H]
        mid = [_bmm(jnp.where(m_off, nmat[h], 0.0).astype(bf16), tb[h]).astype(bf16) for h in H]
        tinv = [tinv[h] + _bmm(tb[h], mid[h]) for h in H]
    tb = [tinv[h].astype(bf16) for h in H]

    aakv = [_bmm(n_aak[h].astype(bf16), jnp.concatenate([zeros_hh, vb[h]], axis=0)).astype(bf16)
            for h in H]
    m1b = [_bmm(tb[h], at[h]).astype(bf16) for h in H]
    u0b = [_bmm(tb[h], aakv[h]).astype(bf16) for h in H]
    uv = [jnp.concatenate([u0b[h], vb[h]], axis=0) for h in H]
    m2 = [rt[h] + _bmm(arb_ark[h][:, :HEAD_DIM], m1b[h]) for h in H]
    y1 = [_bmm(arb_ark[h], uv[h]) for h in H]
    gmat = [eye * g_end[:, hs[h]] + _bmm_tn(bhat[h], m1b[h]) for h in H]
    dmat = [_bmm_tn(jnp.concatenate([bhat[h], khat[h]], axis=0), uv[h]) for h in H]
    upd = [_bmm(jnp.concatenate([m2[h], gmat[h]], axis=0).astype(bf16), st_ref[h].astype(bf16))
           for h in H]

    outs = []
    for h in H:
        sl = hs[h]
        st_ref[h] = upd[h][C:] + dmat[h]
        y = upd[h][:C] + y1[h]
        mean = jnp.mean(y, axis=-1, keepdims=True)
        yc = y - mean
        var = jnp.mean(yc * yc, axis=-1, keepdims=True)
        yn = yc * lax.rsqrt(var + GN_EPS) * lg_ref[:, sl] + lb_ref[:, sl]
        bonus = jnp.sum(rkk[:, sl], axis=-1, keepdims=True) * v[:, sl]
        outs.append((yn + bonus) * g[:, sl])
    o_ref[0] = jnp.concatenate(outs, axis=-1)


def _rwkv(p_rwkv, mu, w0, w_up, a0, a_up, g_up, k_k, k_a, r_k, lnx_g, lnx_b):
    bn, s, _ = p_rwkv.shape
    C = RWKV_CHUNK
    vec = lambda t: t.reshape(1, -1)
    const = lambda b, c: (0, 0)
    full = lambda t: pl.BlockSpec(t.shape, const)
    args = [vec(mu), vec(w0), w_up, vec(a0), a_up, g_up, vec(k_k), vec(k_a), vec(r_k),
            vec(lnx_g), vec(lnx_b)]
    return pl.pallas_call(
        _rwkv_kernel,
        grid=(bn, s // C),
        in_specs=[pl.BlockSpec((1, C, RWKV_COLS), lambda b, c: (b, c, 0))] + [full(t) for t in args],
        out_specs=pl.BlockSpec((1, C, WIDTH), lambda b, c: (b, c, 0)),
        out_shape=jax.ShapeDtypeStruct((bn, s, WIDTH), f32),
        scratch_shapes=[pltpu.VMEM((1, RWKV_COLS), f32),
                        pltpu.VMEM((N_HEADS, HEAD_DIM, HEAD_DIM), f32)],
        compiler_params=_cparams(("parallel", "arbitrary")),
        name="rwkv",
    )(p_rwkv, *args)


def _attn_kernel(q_ref, k_ref, v_ref, bias_ref, o_ref):
    TQ = ATT_QROWS
    t = pl.program_id(1)
    g0 = pl.multiple_of(jnp.maximum(t - PREV_CHUNKS * CHUNK // TQ, 0) * TQ, TQ)
    q = q_ref[0]
    lane = lax.broadcasted_iota(jnp.int32, (1, 128), 1)
    keep_lo = (lane < HEAD_DIM).astype(bf16)
    keep_hi = (lane >= HEAD_DIM).astype(bf16)
    lane_f = lax.broadcasted_iota(jnp.int32, (TQ, 128), 1)
    pairs = range(N_HEADS // 2)

    scores = []
    for p in pairs:
        qp = q[:, p * 128:(p + 1) * 128]
        qs = jnp.concatenate([qp * keep_lo, qp * keep_hi], axis=0)
        kp = k_ref[0, pl.ds(g0, ATT_WINDOW), p * 128:(p + 1) * 128]
        s = lax.dot_general(qs, kp, (((1,), (1,)), ((), ())), preferred_element_type=f32)
        scores.append(s + jnp.concatenate([bias_ref[0, 2 * p], bias_ref[0, 2 * p + 1]], axis=0))
    probs, denoms = [], []
    for p in pairs:
        m = jnp.max(scores[p], axis=-1, keepdims=True)
        e = jnp.exp(scores[p] - m)
        denoms.append(jnp.sum(e, axis=-1, keepdims=True))
        probs.append(e.astype(bf16))
    outs = []
    for p in pairs:
        vp = v_ref[0, pl.ds(g0, ATT_WINDOW), p * 128:(p + 1) * 128]
        o2 = jnp.dot(probs[p], vp, preferred_element_type=f32) / denoms[p]
        outs.append(jnp.where(lane_f < HEAD_DIM, o2[:TQ], o2[TQ:]))
    o_ref[0] = jnp.concatenate(outs, axis=-1)


def _attn_bias_tables(rel_bias):
    TQ, W = ATT_QROWS, ATT_WINDOW
    n_var = PREV_CHUNKS * CHUNK // TQ + 1
    n_h = rel_bias.shape[0]
    rb = rel_bias.astype(f32)
    ext = jnp.concatenate([jnp.broadcast_to(rb[:, :1], (n_h, W - REL_CLIP)), rb,
                           jnp.broadcast_to(rb[:, -1:], (n_h, W - REL_CLIP - 1))], axis=1)
    width = W + (n_var - 1) * TQ
    toep = jnp.stack([ext[:, TQ - qi: TQ - qi + width] for qi in range(TQ)], axis=1)
    qc = np.arange(TQ)[:, None] // CHUNK
    kc = np.arange(W)[None, :] // CHUNK
    tables = []
    for v in range(n_var):
        off = (n_var - 1 - v) * TQ
        rel_chunk = kc - qc - v * (TQ // CHUNK)
        visible = (rel_chunk <= 0) & (rel_chunk >= -PREV_CHUNKS)
        tables.append(jnp.where(jnp.asarray(visible)[None], toep[:, :, off: off + W], NEG_INF))
    return jnp.stack(tables, axis=0)


def _attn(q, k, v, rel_bias):
    bn, s, _ = q.shape
    TQ = ATT_QROWS
    bias = _attn_bias_tables(rel_bias)
    n_var = bias.shape[0]
    tile = pl.BlockSpec((1, TQ, WIDTH), lambda b, t: (b, t, 0))
    whole = pl.BlockSpec((1, s, WIDTH), lambda b, t: (b, 0, 0))
    return pl.pallas_call(
        _attn_kernel,
        grid=(bn, s // TQ),
        in_specs=[tile, whole, whole,
                  pl.BlockSpec((1, N_HEADS, TQ, ATT_WINDOW),
                               lambda b, t: (jnp.minimum(t, n_var - 1), 0, 0, 0))],
        out_specs=tile,
        out_shape=jax.ShapeDtypeStruct((bn, s, WIDTH), f32),
        compiler_params=_cparams(("parallel", "arbitrary")),
        name="attn",
    )(q, k, v, bias)


def _store_tile_rows(ref, x):
    rows = x.shape[0]
    for j in range(ROW_TILES):
        ref[pl.ds(j, rows, stride=ROW_TILES), :] = x[:, j * 128:(j + 1) * 128]


def _load_tile_rows(ref, rows):
    return jnp.concatenate(
        [ref[pl.ds(j, rows, stride=ROW_TILES), :] for j in range(ROW_TILES)], axis=1)


def _route(logits, tri, counts):
    lane = lax.broadcasted_iota(jnp.int32, logits.shape, 1)
    big = jnp.int32(ROUTE_LANES)
    is_c = lane < N_GROUPS
    cl = jnp.where(is_c, logits, NEG_INF)
    cm = jnp.max(cl, axis=-1, keepdims=True)
    grp = jnp.min(jnp.where(cl == cm, lane, big), axis=-1, keepdims=True)
    p_grp = 1.0 / jnp.sum(jnp.where(is_c, jnp.exp(cl - cm), 0.0), axis=-1, keepdims=True)
    lo = N_GROUPS + grp * EXPERTS_PER_GROUP
    in_g = (lane >= lo) & (lane < lo + EXPERTS_PER_GROUP)
    fl = jnp.where(in_g, logits, NEG_INF)
    f1 = jnp.max(fl, axis=-1, keepdims=True)
    i1 = jnp.min(jnp.where(in_g & (fl == f1), lane, big), axis=-1, keepdims=True)
    fl2 = jnp.where(lane == i1, NEG_INF, fl)
    in_g2 = in_g & (lane != i1)
    f2 = jnp.max(fl2, axis=-1, keepdims=True)
    i2 = jnp.min(jnp.where(in_g2 & (fl2 == f2), lane, big), axis=-1, keepdims=True)
    e2 = jnp.exp(f2 - f1)
    w1 = p_grp / (1.0 + e2)
    w2 = p_grp * e2 / (1.0 + e2)

    hot1 = (lane == i1).astype(f32)
    hot2 = (lane == i2).astype(f32)
    pre1 = jnp.dot(tri, hot1.astype(bf16), preferred_element_type=f32)
    pre2 = jnp.dot(tri, hot2.astype(bf16), preferred_element_type=f32)
    tot1 = jnp.sum(hot1, axis=0, keepdims=True)
    tot2 = jnp.sum(hot2, axis=0, keepdims=True)
    rank1 = jnp.sum(hot1 * (pre1 + counts), axis=-1, keepdims=True)
    rank2 = jnp.sum(hot2 * (pre2 + counts + tot1), axis=-1, keepdims=True)

    out = jnp.where(lane == 0, w1, 0.0)
    out = jnp.where(lane == 1, w2, out)
    out = jnp.where(lane == 2, (i1 - N_GROUPS).astype(f32), out)
    out = jnp.where(lane == 3, (i2 - N_GROUPS).astype(f32), out)
    out = jnp.where(lane == 4, rank1, out)
    out = jnp.where(lane == 5, rank2, out)
    return out, counts + tot1 + tot2


def _merge_kernel(yr_ref, ya_ref, gt_ref, x_ref, g1_ref, sh_ref, sc_ref, ng_ref, wbr_ref,
                  wba_ref, wout_ref, wrt_ref, brt_ref, tri_ref, x1_ref, h2_ref, route_ref,
                  cnt_ref):
    @pl.when((pl.program_id(0) == 0) & (pl.program_id(1) == 0))
    def _():
        cnt_ref[...] = jnp.zeros_like(cnt_ref)

    t1 = jnp.dot(yr_ref[0].astype(bf16), wbr_ref[...], preferred_element_type=f32)
    t2 = jnp.dot(ya_ref[0].astype(bf16), wba_ref[...], preferred_element_type=f32)
    gt = gt_ref[0]
    m = gt[:, :D_MODEL] * t1 + gt[:, D_MODEL:] * t2
    mixed = jnp.dot(m.astype(bf16), wout_ref[...], preferred_element_type=f32)
    x1 = x_ref[0] + g1_ref[0] * mixed
    x1_ref[0] = x1
    h2 = _rms_mod(x1, ng_ref[...], sh_ref[0], sc_ref[0])
    _store_tile_rows(h2_ref.at[0], h2)
    logits = jnp.dot(h2, wrt_ref[...], precision=HI, preferred_element_type=f32) + brt_ref[...]
    route, counts = _route(logits, tri_ref[...], cnt_ref[...])
    route_ref[0] = route
    cnt_ref[...] = counts


def _merge(y_r, y_a, gates, x, g1, sh2, sc2, norm2_g, w_br, w_ba, w_out, w_rt, b_rt):
    bn, s, d = x.shape
    tm = min(MERGE_ROWS, s)
    row = lambda b, i: (b, i, 0)
    per_b = lambda b, i: (b, 0, 0)
    const = lambda b, i: (0, 0)
    full = lambda t: pl.BlockSpec(t.shape, const)
    tri = jnp.asarray(np.tril(np.ones((tm, tm), np.float32), -1), bf16)
    return pl.pallas_call(
        _merge_kernel,
        grid=(bn, s // tm),
        in_specs=[pl.BlockSpec((1, tm, WIDTH), row),
                  pl.BlockSpec((1, tm, WIDTH), row),
                  pl.BlockSpec((1, tm, GATE_COLS), row),
                  pl.BlockSpec((1, tm, d), row),
                  pl.BlockSpec((1, 1, d), per_b),
                  pl.BlockSpec((1, 1, d), per_b),
                  pl.BlockSpec((1, 1, d), per_b),
                  pl.BlockSpec((1, d), const),
                  full(w_br), full(w_ba), full(w_out), full(w_rt), full(b_rt), full(tri)],
        out_specs=[pl.BlockSpec((1, tm, d), row),
                   pl.BlockSpec((1, tm * ROW_TILES, 128), row),
                   pl.BlockSpec((1, tm, ROUTE_LANES), row),
                   pl.BlockSpec((1, ROUTE_LANES), const)],
        out_shape=[jax.ShapeDtypeStruct((bn, s, d), f32),
                   jax.ShapeDtypeStruct((bn, s * ROW_TILES, 128), f32),
                   jax.ShapeDtypeStruct((bn, s, ROUTE_LANES), f32),
                   jax.ShapeDtypeStruct((1, ROUTE_LANES), f32)],
        compiler_params=_cparams(("arbitrary", "arbitrary")),
        name="merge",
    )(y_r, y_a, gates, x, g1.reshape(bn, 1, d), sh2.reshape(bn, 1, d), sc2.reshape(bn, 1, d),
      norm2_g.reshape(1, d), w_br, w_ba, w_out, w_rt, b_rt, tri)


def _row_copy(src, src_row, dst, dst_row, sem):
    return pltpu.make_async_copy(src.at[src_row], dst.at[pl.ds(dst_row * ROW_TILES, ROW_TILES)],
                                 sem)


def _moe_kernel(tok_ref, be_ref, j0_ref, h2_hbm, wg_ref, wu_ref, wd_ref, o_ref, xbuf, sem):
    del be_ref
    i = pl.program_id(0)
    last = pl.num_programs(0) - 1
    n_asg = tok_ref.shape[0]

    def issue(blk, slot):
        j0 = j0_ref[blk]
        for r in range(MOE_ROWS):
            tok = tok_ref[jnp.minimum(j0 + r, n_asg - 1)]
            _row_copy(h2_hbm, tok, xbuf.at[slot], r, sem.at[slot]).start()

    def wait(slot):
        for r in range(MOE_ROWS):
            _row_copy(h2_hbm, 0, xbuf.at[slot], r, sem.at[slot]).wait()

    for slot in range(2):
        @pl.when(i % 2 == slot)
        def _(slot=slot):
            if slot == 0:
                @pl.when(i == 0)
                def _():
                    issue(0, 0)

            wait(slot)
            issue(jnp.minimum(i + 1, last), 1 - slot)
            xb = _load_tile_rows(xbuf.at[slot], MOE_ROWS).astype(bf16)
            gate = jnp.dot(xb, wg_ref[0], preferred_element_type=f32)
            up = jnp.dot(xb, wu_ref[0], preferred_element_type=f32)
            hid = gate * _sigmoid(gate) * up
            _store_tile_rows(
                o_ref, jnp.dot(hid.astype(bf16), wd_ref[0], preferred_element_type=f32))

            @pl.when(i == last)
            def _():
                wait(1 - slot)


def _moe(h2_tiles, tok_sorted, block_e, block_j0, wg, wu, wd):
    d = D_MODEL
    n_blocks = block_e.shape[0]
    blk_rows = MOE_ROWS * ROW_TILES
    by_expert = lambda i, tok, be, j0: (be[i], 0, 0)
    grid_spec = pltpu.PrefetchScalarGridSpec(
        num_scalar_prefetch=3,
        grid=(n_blocks,),
        in_specs=[pl.BlockSpec(memory_space=pl.ANY),
                  pl.BlockSpec((1, d, D_EXPERT), by_expert),
                  pl.BlockSpec((1, d, D_EXPERT), by_expert),
                  pl.BlockSpec((1, D_EXPERT, d), by_expert)],
        out_specs=pl.BlockSpec((blk_rows, 128), lambda i, tok, be, j0: (i, 0)),
        scratch_shapes=[pltpu.VMEM((2, blk_rows, 128), f32), pltpu.SemaphoreType.DMA((2,))],
    )
    yb = pl.pallas_call(
        _moe_kernel,
        grid_spec=grid_spec,
        out_shape=jax.ShapeDtypeStruct((n_blocks * blk_rows, 128), f32),
        compiler_params=_cparams(("arbitrary",)),
        name="moe",
    )(tok_sorted, block_e, block_j0, h2_tiles, wg, wu, wd)
    return yb.reshape(n_blocks * MOE_ROWS, ROW_TILES, 128)


def _combine_kernel(d0_ref, d1_ref, yb_hbm, x1_ref, rt_ref, g2_ref, o_ref, buf, sem):
    tc = buf.shape[2] // ROW_TILES
    step = pl.program_id(0) * pl.num_programs(1) + pl.program_id(1)
    n_steps = pl.num_programs(0) * pl.num_programs(1)

    def issue(for_step, slot):
        base = for_step * tc
        for r in range(tc):
            _row_copy(yb_hbm, d0_ref[base + r], buf.at[slot, 0], r, sem.at[slot]).start()
            _row_copy(yb_hbm, d1_ref[base + r], buf.at[slot, 1], r, sem.at[slot]).start()

    def wait(slot):
        for r in range(tc):
            _row_copy(yb_hbm, 0, buf.at[slot, 0], r, sem.at[slot]).wait()
            _row_copy(yb_hbm, 0, buf.at[slot, 1], r, sem.at[slot]).wait()

    @pl.when(step == 0)
    def _():
        issue(step, 0)

    for slot in range(2):
        @pl.when(step % 2 == slot)
        def _(slot=slot):
            @pl.when(step + 1 < n_steps)
            def _():
                issue(step + 1, 1 - slot)

            wait(slot)
            rt = rt_ref[0]
            moe = (rt[:, 0:1] * _load_tile_rows(buf.at[slot, 0], tc)
                   + rt[:, 1:2] * _load_tile_rows(buf.at[slot, 1], tc))
            o_ref[0] = x1_ref[0] + g2_ref[0] * moe


def _combine(dest0, dest1, yb, x1, route, g2):
    bn, s, d = x1.shape
    tc = min(COMBINE_ROWS, s)
    row = lambda b, i, d0, d1: (b, i, 0)
    grid_spec = pltpu.PrefetchScalarGridSpec(
        num_scalar_prefetch=2,
        grid=(bn, s // tc),
        in_specs=[pl.BlockSpec(memory_space=pl.ANY),
                  pl.BlockSpec((1, tc, d), row),
                  pl.BlockSpec((1, tc, ROUTE_LANES), row),
                  pl.BlockSpec((1, 1, d), lambda b, i, d0, d1: (b, 0, 0))],
        out_specs=pl.BlockSpec((1, tc, d), row),
        scratch_shapes=[pltpu.VMEM((2, 2, tc * ROW_TILES, 128), f32),
                        pltpu.SemaphoreType.DMA((2,))],
    )
    return pl.pallas_call(
        _combine_kernel,
        grid_spec=grid_spec,
        out_shape=jax.ShapeDtypeStruct((bn, s, d), f32),
        compiler_params=_cparams(("arbitrary", "arbitrary")),
        name="combine",
    )(dest0, dest1, yb, x1, route, g2.reshape(bn, 1, d))


def _dispatch_plan(route, counts):
    n_tok = route.shape[0]
    counts = counts.astype(jnp.int32)
    padded = (counts + MOE_ROWS - 1) // MOE_ROWS * MOE_ROWS
    pad_ends = jnp.cumsum(padded)
    pad_starts = pad_ends - padded
    starts = jnp.cumsum(counts) - counts
    n_blocks = (2 * n_tok + N_EXPERTS * MOE_ROWS) // MOE_ROWS
    eid = route[:, 2:4].astype(jnp.int32)
    rank = route[:, 4:6].astype(jnp.int32)
    hot = eid[:, :, None] == jnp.arange(N_EXPERTS, dtype=jnp.int32)
    dest = rank + jnp.sum(jnp.where(hot, pad_starts, 0), axis=-1)
    tok = jnp.broadcast_to(jnp.arange(n_tok, dtype=jnp.int32)[:, None], (n_tok, 2))
    _, tok_sorted = lax.sort((dest.reshape(-1), tok.reshape(-1)), num_keys=1)
    block_row = jnp.arange(n_blocks, dtype=jnp.int32) * MOE_ROWS
    block_e = jnp.minimum(jnp.sum(pad_ends[None, :] <= block_row[:, None], axis=-1),
                          N_EXPERTS - 1).astype(jnp.int32)
    hot_b = block_e[:, None] == jnp.arange(N_EXPERTS, dtype=jnp.int32)
    block_j0 = block_row + jnp.sum(jnp.where(hot_b, starts - pad_starts, 0), axis=-1)
    block_j0 = jnp.clip(block_j0, 0, 2 * n_tok - 1).astype(jnp.int32)
    return tok_sorted, block_e, block_j0, dest[:, 0], dest[:, 1]


def kernel(x, c, w_ada, b_ada, norm1_g, w_in, rwkv_mu, rwkv_w0, rwkv_w_up, rwkv_a0, rwkv_a_up, rwkv_g_up, rwkv_k_k, rwkv_k_a, rwkv_r_k, rwkv_lnx_g, rwkv_lnx_b, attn_q_g, attn_k_g, attn_rel_bias, w_branch_rwkv, w_branch_attn, w_out, norm2_g, router_coarse_w, router_coarse_b, router_fine_w, router_fine_b, expert_w_gate, expert_w_up, expert_w_down):
    bn, s, d = x.shape
    depth = w_ada.shape[0]
    for l in range(depth):
        mod = _ada(c, w_ada[l], b_ada[l])
        sh1, sc1, g1, sh2, sc2, g2 = [mod[:, i * d:(i + 1) * d] for i in range(6)]

        p_rwkv, q, k, v, gates = _inproj(x, norm1_g[l], sh1, sc1, w_in[l].astype(bf16),
                                         attn_q_g[l], attn_k_g[l])
        y_r = _rwkv(p_rwkv, rwkv_mu[l], rwkv_w0[l], rwkv_w_up[l], rwkv_a0[l], rwkv_a_up[l],
                    rwkv_g_up[l], rwkv_k_k[l], rwkv_k_a[l], rwkv_r_k[l].reshape(-1),
                    rwkv_lnx_g[l], rwkv_lnx_b[l])
        y_a = _attn(q, k, v, attn_rel_bias[l])

        n_rt = N_GROUPS + N_EXPERTS
        w_rt = jnp.zeros((d, ROUTE_LANES), f32)
        w_rt = w_rt.at[:, :N_GROUPS].set(router_coarse_w[l]).at[:, N_GROUPS:n_rt].set(router_fine_w[l])
        b_rt = jnp.zeros((1, ROUTE_LANES), f32)
        b_rt = b_rt.at[0, :N_GROUPS].set(router_coarse_b[l]).at[0, N_GROUPS:n_rt].set(router_fine_b[l])
        x1, h2, route, counts = _merge(y_r, y_a, gates, x, g1, sh2, sc2, norm2_g[l],
                                       w_branch_rwkv[l].astype(bf16),
                                       w_branch_attn[l].astype(bf16),
                                       w_out[l].astype(bf16), w_rt, b_rt)

        tok_sorted, block_e, block_j0, dest0, dest1 = _dispatch_plan(
            route.reshape(bn * s, ROUTE_LANES), counts[0, N_GROUPS:n_rt])
        yb = _moe(h2.reshape(bn * s, ROW_TILES, 128), tok_sorted, block_e, block_j0,
                  expert_w_gate[l].astype(bf16), expert_w_up[l].astype(bf16),
                  expert_w_down[l].astype(bf16))
        x = _combine(dest0, dest1, yb, x1, route, g2)
    return x
```

```python
import functools

import numpy as np
import jax
import jax.numpy as jnp
from jax import lax
from jax.experimental import pallas as pl
from jax.experimental.pallas import tpu as pltpu

f32 = jnp.float32
bf16 = jnp.bfloat16
HI = lax.Precision.HIGHEST

D_MODEL = 1024
HEAD_DIM = 64
N_HEADS = 8
WIDTH = N_HEADS * HEAD_DIM
DECAY_LORA = 64
ICLR_LORA = 64
GATE_LORA = 128
RWKV_COLS = 3 * WIDTH + DECAY_LORA + ICLR_LORA + GATE_LORA
ATT_COLS = 3 * WIDTH
GATE_COLS = 2 * D_MODEL
CHUNK = 64
PREV_CHUNKS = 8
REL_CLIP = 128
N_GROUPS = 4
EXPERTS_PER_GROUP = 8
N_EXPERTS = N_GROUPS * EXPERTS_PER_GROUP
D_EXPERT = 512
NORM_EPS = 1e-6
GN_EPS = 64e-5
NEG_INF = -1e30

INPROJ_ROWS = 256
RWKV_CHUNK = 64
RWKV_BATCH_ROWS = 4
ATT_QROWS = 128
ATT_WINDOW = PREV_CHUNKS * CHUNK + ATT_QROWS
MERGE_ROWS = 512
MOE_ROWS = 512
COMBINE_ROWS = 128
ROUTE_LANES = 128
ROW_TILES = D_MODEL // 128
VMEM_LIMIT = 56 * 1024 * 1024


def _cparams(sem):
    return pltpu.CompilerParams(dimension_semantics=sem, vmem_limit_bytes=VMEM_LIMIT)


def _sigmoid(x):
    return 1.0 / (1.0 + jnp.exp(-x))


def _ada_kernel(c_ref, w_ref, b_ref, o_ref):
    c = c_ref[...]
    s = c * _sigmoid(c)
    o_ref[...] = jnp.dot(s, w_ref[...], precision=HI, preferred_element_type=f32) + b_ref[...]


def _ada(c, w, b):
    bn, d = c.shape
    n = w.shape[1]
    tn = 1024
    return pl.pallas_call(
        _ada_kernel,
        grid=(n // tn,),
        in_specs=[pl.BlockSpec((bn, d), lambda j: (0, 0)),
                  pl.BlockSpec((d, tn), lambda j: (0, j)),
                  pl.BlockSpec((1, tn), lambda j: (0, j))],
        out_specs=pl.BlockSpec((bn, tn), lambda j: (0, j)),
        out_shape=jax.ShapeDtypeStruct((bn, n), f32),
        compiler_params=_cparams(("arbitrary",)),
        name="ada",
    )(c, w, b.reshape(1, n))


def _rms_mod(x, gain, shift, scale):
    ms = jnp.mean(x * x, axis=-1, keepdims=True)
    h = x * lax.rsqrt(ms + NORM_EPS) * gain
    return h * (1.0 + scale) + shift


def _head_norm(x, gain, ones_bd):
    outs = []
    for c in range(WIDTH // 128):
        xc = x[:, c * 128:(c + 1) * 128]
        ms = jnp.dot((xc * xc).astype(bf16), ones_bd, preferred_element_type=f32) * (1.0 / HEAD_DIM)
        outs.append(xc * lax.rsqrt(ms + NORM_EPS))
    return jnp.concatenate(outs, axis=1) * gain


def _inproj_kernel(x_ref, g_ref, sh_ref, sc_ref, w_ref, qg_ref, kg_ref, bd_ref,
                   pr_ref, q_ref, k_ref, v_ref, pg_ref):
    h = _rms_mod(x_ref[0], g_ref[...], sh_ref[0], sc_ref[0]).astype(bf16)
    c1, c2 = RWKV_COLS, RWKV_COLS + ATT_COLS
    pr_ref[0] = jnp.dot(h, w_ref[:, :c1], preferred_element_type=f32)
    pa = jnp.dot(h, w_ref[:, c1:c2], preferred_element_type=f32)
    ones_bd = bd_ref[...]
    q_ref[0] = _head_norm(pa[:, :WIDTH], qg_ref[...], ones_bd).astype(bf16)
    k_ref[0] = _head_norm(pa[:, WIDTH:2 * WIDTH], kg_ref[...], ones_bd).astype(bf16)
    v_ref[0] = pa[:, 2 * WIDTH:].astype(bf16)
    pg_ref[0] = _sigmoid(jnp.dot(h, w_ref[:, c2:], preferred_element_type=f32))


def _inproj(x, gain, shift, scale, w_bf, q_g, k_g):
    bn, s, d = x.shape
    tm = min(INPROJ_ROWS, s)
    n = w_bf.shape[1]
    row = lambda b, i: (b, i, 0)
    per_b = lambda b, i: (b, 0, 0)
    const = lambda b, i: (0, 0)
    per_head = lambda g, scale: (jnp.tile(g.astype(f32), N_HEADS) * scale).reshape(1, WIDTH)
    lane_head = np.arange(128) // HEAD_DIM
    ones_bd = jnp.asarray(lane_head[:, None] == lane_head[None, :], bf16)
    return pl.pallas_call(
        _inproj_kernel,
        grid=(bn, s // tm),
        in_specs=[pl.BlockSpec((1, tm, d), row),
                  pl.BlockSpec((1, d), const),
                  pl.BlockSpec((1, 1, d), per_b),
                  pl.BlockSpec((1, 1, d), per_b),
                  pl.BlockSpec((d, n), const),
                  pl.BlockSpec((1, WIDTH), const),
                  pl.BlockSpec((1, WIDTH), const),
                  pl.BlockSpec((128, 128), const)],
        out_specs=[pl.BlockSpec((1, tm, RWKV_COLS), row),
                   pl.BlockSpec((1, tm, WIDTH), row),
                   pl.BlockSpec((1, tm, WIDTH), row),
                   pl.BlockSpec((1, tm, WIDTH), row),
                   pl.BlockSpec((1, tm, GATE_COLS), row)],
        out_shape=[jax.ShapeDtypeStruct((bn, s, RWKV_COLS), f32),
                   jax.ShapeDtypeStruct((bn, s, WIDTH), bf16),
                   jax.ShapeDtypeStruct((bn, s, WIDTH), bf16),
                   jax.ShapeDtypeStruct((bn, s, WIDTH), bf16),
                   jax.ShapeDtypeStruct((bn, s, GATE_COLS), f32)],
        compiler_params=_cparams(("parallel", "parallel")),
        name="inproj",
    )(x, gain.reshape(1, d), shift.reshape(bn, 1, d), scale.reshape(bn, 1, d), w_bf,
      per_head(q_g, HEAD_DIM ** -0.5), per_head(k_g, 1.0), ones_bd)


def _split_bf16(x):
    hi = x.astype(bf16)
    return hi, (x - hi.astype(f32)).astype(bf16)


def _mm_split(x, w2):
    rows, n = x.shape[0], w2.shape[1] // 2
    pr = jnp.dot(jnp.concatenate(_split_bf16(x), axis=0), w2, preferred_element_type=f32)
    return (pr[:rows, :n] + pr[rows:, :n]) + (pr[:rows, n:] + pr[rows:, n:])


def _bmm(a, b):
    return jnp.dot(a, b, preferred_element_type=f32)


def _bmm_nt(a, b):
    return lax.dot_general(a, b, (((1,), (1,)), ((), ())), preferred_element_type=f32)


def _bmm_tn(a, b):
    return lax.dot_general(a, b, (((0,), (0,)), ((), ())), preferred_element_type=f32)


def _rwkv_kernel(p_ref, mu_ref, w0_ref, wup_ref, a0_ref, aup_ref, gup_ref, kk_ref, ka_ref,
                 rk_ref, lg_ref, lb_ref, o_ref, last_ref, st_ref):
    C = RWKV_CHUNK
    c_idx = pl.program_id(1)

    @pl.when(c_idx == 0)
    def _():
        last_ref[...] = jnp.zeros_like(last_ref)
        st_ref[...] = jnp.zeros_like(st_ref)

    NB = p_ref.shape[0]
    R = NB * C
    rows_of = [slice(b * C, (b + 1) * C) for b in range(NB)]
    pc = p_ref[...].reshape(R, RWKV_COLS)
    row_full = lax.broadcasted_iota(jnp.int32, pc.shape, 0)
    prev = pltpu.roll(pc, 1, axis=0)
    for b in range(NB):
        prev = jnp.where(row_full == b * C, last_ref[b:b + 1, :], prev)
        last_ref[b:b + 1, :] = pc[(b + 1) * C - 1:(b + 1) * C, :]
    ps = pc + (prev - pc) * mu_ref[...]

    o1, o2, o3 = WIDTH, 2 * WIDTH, 3 * WIDTH
    o4, o5 = o3 + DECAY_LORA, o3 + DECAY_LORA + ICLR_LORA
    r, k, v = ps[:, :o1], ps[:, o1:o2], ps[:, o2:o3]
    wd, ad, gd = ps[:, o3:o4], ps[:, o4:o5], ps[:, o5:]

    wraw = w0_ref[...] + _mm_split(jnp.tanh(wd), wup_ref[...])
    z = -wraw
    softplus = jnp.maximum(z, 0.0) + jnp.log(1.0 + jnp.exp(-jnp.abs(z)))
    logdec = -jnp.exp(-softplus - 0.5)
    a = _sigmoid(a0_ref[...] + _mm_split(ad, aup_ref[...]))
    g = _mm_split(_sigmoid(gd), gup_ref[...])
    kkr = k * kk_ref[...]
    k2 = k * (1.0 + (a - 1.0) * ka_ref[...])
    rkk = r * k2 * rk_ref[...]

    ri = lax.broadcasted_iota(jnp.int32, (R, R), 0)
    ci = lax.broadcasted_iota(jnp.int32, (R, R), 1)
    tri = ((ri >= ci) & (ri // C == ci // C)).astype(bf16)
    ld_hi = logdec.astype(bf16)
    ld_r = logdec - ld_hi.astype(f32)
    ld_mid = ld_r.astype(bf16)
    ld_lo = (ld_r - ld_mid.astype(f32)).astype(bf16)
    cum3 = _bmm(tri, jnp.concatenate([ld_hi, ld_mid, ld_lo], axis=1))
    cum = cum3[:, :WIDTH] + cum3[:, WIDTH:2 * WIDTH] + cum3[:, 2 * WIDTH:]
    e_pos = jnp.exp(cum)
    e_prev = jnp.exp(cum - logdec)
    e_neg = jnp.exp(-cum)
    cum_ends = [cum[(b + 1) * C - 1:(b + 1) * C, :] for b in range(NB)]
    e_end = jnp.exp(jnp.concatenate([jnp.broadcast_to(ce, (C, WIDTH)) for ce in cum_ends], axis=0)
                    - cum)
    g_end = [jnp.exp(ce) for ce in cum_ends]

    C2 = 2 * C
    lane1 = lax.broadcasted_iota(jnp.int32, (1, 128), 1)
    keep_lo = (lane1 < HEAD_DIM).astype(bf16)
    keep_hi = (lane1 >= HEAD_DIM).astype(bf16)

    def bd(x):
        return jnp.concatenate([x * keep_lo, x * keep_hi], axis=0)

    r2 = lax.broadcasted_iota(jnp.int32, (C2, C2), 0)
    c2 = lax.broadcasted_iota(jnp.int32, (C2, C2), 1)
    ones_bd = ((r2 // HEAD_DIM) == (c2 // HEAD_DIM)).astype(bf16)
    eye = (r2 == c2).astype(f32)
    rr, cc = r2 % C, c2 % C
    strict = rr > cc
    blk16 = (rr // 16) == (cc // 16)
    blk32 = (rr // 32) == (cc // 32)
    m_diag16 = strict & blk16
    m_off32 = strict & blk32 & jnp.logical_not(blk16)
    m_off64 = strict & jnp.logical_not(blk32)
    r4 = lax.broadcasted_iota(jnp.int32, (C2, 2 * C2), 0) % C
    c4 = lax.broadcasted_iota(jnp.int32, (C2, 2 * C2), 1) % C
    strict2 = r4 > c4
    incl2 = r4 >= c4
    zeros_bd = jnp.zeros((C2, C2), bf16)

    P = range(N_HEADS // 2)
    ps_ = [slice(p * 128, (p + 1) * 128) for p in P]

    def head_sums(x):
        return jnp.concatenate([_bmm(x[:, s], ones_bd) for s in ps_], axis=1)

    kkn = kkr * lax.rsqrt(jnp.maximum(head_sums((kkr * kkr).astype(bf16)), 1e-24))
    bh = kkn * a
    at_w = (-kkn * e_prev).astype(bf16)
    rt_w = (r * e_pos).astype(bf16)
    bt_w = (bh * e_neg).astype(bf16)
    kt_w = (k2 * e_neg).astype(bf16)
    bhat_w = (bh * e_end).astype(bf16)
    khat_w = (k2 * e_end).astype(bf16)
    v_w = v.astype(bf16)

    units = [(b, p) for b in range(NB) for p in P]
    P = range(len(units))

    def sub(x, q):
        return x[rows_of[units[q][0]], ps_[units[q][1]]]

    at = [bd(sub(at_w, q)) for q in P]
    rt = [bd(sub(rt_w, q)) for q in P]
    vb = [bd(sub(v_w, q)) for q in P]
    smat = [_bmm_nt(jnp.concatenate([at[p], rt[p]], axis=0),
                    jnp.concatenate([bd(sub(bt_w, p)), bd(sub(kt_w, p))], axis=0)) for p in P]
    n_aak = [jnp.where(strict2, smat[p][:C2], 0.0) for p in P]
    arb_ark = [jnp.where(incl2, smat[p][C2:], 0.0).astype(bf16) for p in P]
    nmat = [n_aak[p][:, :C2] for p in P]

    pw = [jnp.where(m_diag16, nmat[p], 0.0) for p in P]
    tinv = [eye + pw[p] for p in P]
    for _ in range(3):
        pwb = [pw[p].astype(bf16) for p in P]
        pw = [_bmm(pwb[p], pwb[p]) for p in P]
        tinv = [tinv[p] + _bmm(tinv[p].astype(bf16), pw[p].astype(bf16)) for p in P]
    for m_off in (m_off32, m_off64):
        tb = [tinv[p].astype(bf16) for p in P]
        mid = [_bmm(jnp.where(m_off, nmat[p], 0.0).astype(bf16), tb[p]).astype(bf16) for p in P]
        tinv = [tinv[p] + _bmm(tb[p], mid[p]) for p in P]
    tb = [tinv[p].astype(bf16) for p in P]

    aakv = [_bmm(n_aak[p][:, C2:].astype(bf16), vb[p]).astype(bf16) for p in P]
    m1u0 = [_bmm(tb[p], jnp.concatenate([at[p], aakv[p]], axis=1)).astype(bf16) for p in P]
    rhs = [jnp.concatenate([m1u0[p], jnp.concatenate([zeros_bd, vb[p]], axis=1)], axis=0)
           for p in P]
    my = [_bmm(arb_ark[p], rhs[p]) for p in P]
    gd_ = [_bmm_tn(jnp.concatenate([bd(sub(bhat_w, p)), bd(sub(khat_w, p))], axis=0), rhs[p])
           for p in P]
    m2 = [(rt[p].astype(f32) + my[p][:, :C2]).astype(bf16) for p in P]
    gmat = [(eye * g_end[units[p][0]][:, ps_[units[p][1]]] + gd_[p][:, :C2]).astype(bf16)
            for p in P]
    upd = [_bmm(jnp.concatenate([m2[p], gmat[p]], axis=0), st_ref[p].astype(bf16)) for p in P]

    lane_c = lax.broadcasted_iota(jnp.int32, (C, 128), 1)
    ys = []
    for p in P:
        st_ref[p] = upd[p][C2:] + gd_[p][:, C2:]
        y_bd = upd[p][:C2] + my[p][:, C2:]
        ys.append(jnp.where(lane_c < HEAD_DIM, y_bd[:C], y_bd[C:]))
    n_pairs = len(ps_)
    y = jnp.concatenate([jnp.concatenate(ys[b * n_pairs:(b + 1) * n_pairs], axis=1)
                         for b in range(NB)], axis=0)

    y_hi = y.astype(bf16)
    y_lo = (y - y_hi.astype(f32)).astype(bf16)
    sums = head_sums(jnp.concatenate([y_hi, y_lo, rkk.astype(bf16)], axis=0))
    yc = y - (sums[:R] + sums[R:2 * R]) * (1.0 / HEAD_DIM)
    var = head_sums((yc * yc).astype(bf16)) * (1.0 / HEAD_DIM)
    yn = yc * lax.rsqrt(var + GN_EPS) * lg_ref[...] + lb_ref[...]
    o_ref[...] = ((yn + sums[2 * R:] * v) * g).reshape(NB, C, WIDTH)


def _rwkv(p_rwkv, mu, w0, w_up, a0, a_up, g_up, k_k, k_a, r_k, lnx_g, lnx_b):
    bn, s, _ = p_rwkv.shape
    C = RWKV_CHUNK
    vec = lambda t: t.reshape(1, -1)
    const = lambda b, c: (0, 0)
    full = lambda t: pl.BlockSpec(t.shape, const)
    hi_lo = lambda w: jnp.concatenate(_split_bf16(w.astype(f32)), axis=1)
    args = [vec(mu), vec(w0), hi_lo(w_up), vec(a0), hi_lo(a_up), hi_lo(g_up), vec(k_k), vec(k_a),
            vec(r_k), vec(lnx_g), vec(lnx_b)]
    nb = RWKV_BATCH_ROWS if bn % RWKV_BATCH_ROWS == 0 else 1
    return pl.pallas_call(
        _rwkv_kernel,
        grid=(bn // nb, s // C),
        in_specs=[pl.BlockSpec((nb, C, RWKV_COLS), lambda b, c: (b, c, 0))] + [full(t) for t in args],
        out_specs=pl.BlockSpec((nb, C, WIDTH), lambda b, c: (b, c, 0)),
        out_shape=jax.ShapeDtypeStruct((bn, s, WIDTH), f32),
        scratch_shapes=[pltpu.VMEM((nb, RWKV_COLS), f32),
                        pltpu.VMEM((nb * N_HEADS // 2, 2 * HEAD_DIM, 2 * HEAD_DIM), f32)],
        compiler_params=_cparams(("parallel", "arbitrary")),
        name="rwkv",
    )(p_rwkv, *args)


def _attn_kernel(q_ref, k_ref, v_ref, bias_ref, o_ref):
    TQ = ATT_QROWS
    t = pl.program_id(1)
    g0 = pl.multiple_of(jnp.maximum(t - PREV_CHUNKS * CHUNK // TQ, 0) * TQ, TQ)
    q = q_ref[0]
    lane = lax.broadcasted_iota(jnp.int32, (1, 128), 1)
    keep_lo = (lane < HEAD_DIM).astype(bf16)
    keep_hi = (lane >= HEAD_DIM).astype(bf16)
    lane_f = lax.broadcasted_iota(jnp.int32, (TQ, 128), 1)
    pairs = range(N_HEADS // 2)

    scores = []
    for p in pairs:
        qp = q[:, p * 128:(p + 1) * 128]
        qs = jnp.concatenate([qp * keep_lo, qp * keep_hi], axis=0)
        kp = k_ref[0, pl.ds(g0, ATT_WINDOW), p * 128:(p + 1) * 128]
        s = lax.dot_general(qs, kp, (((1,), (1,)), ((), ())), preferred_element_type=f32)
        scores.append(s + jnp.concatenate([bias_ref[0, 2 * p], bias_ref[0, 2 * p + 1]], axis=0))
    probs, denoms = [], []
    for p in pairs:
        m = jnp.max(scores[p], axis=-1, keepdims=True)
        e = jnp.exp(scores[p] - m)
        denoms.append(jnp.sum(e, axis=-1, keepdims=True))
        probs.append(e.astype(bf16))
    outs = []
    for p in pairs:
        vp = v_ref[0, pl.ds(g0, ATT_WINDOW), p * 128:(p + 1) * 128]
        o2 = jnp.dot(probs[p], vp, preferred_element_type=f32) / denoms[p]
        outs.append(jnp.where(lane_f < HEAD_DIM, o2[:TQ], o2[TQ:]))
    o_ref[0] = jnp.concatenate(outs, axis=-1)


def _attn_bias_tables(rel_bias):
    TQ, W = ATT_QROWS, ATT_WINDOW
    n_var = PREV_CHUNKS * CHUNK // TQ + 1
    n_h = rel_bias.shape[0]
    rb = rel_bias.astype(f32)
    ext = jnp.concatenate([jnp.broadcast_to(rb[:, :1], (n_h, W - REL_CLIP)), rb,
                           jnp.broadcast_to(rb[:, -1:], (n_h, W - REL_CLIP - 1))], axis=1)
    width = W + (n_var - 1) * TQ
    toep = jnp.stack([ext[:, TQ - qi: TQ - qi + width] for qi in range(TQ)], axis=1)
    qc = np.arange(TQ)[:, None] // CHUNK
    kc = np.arange(W)[None, :] // CHUNK
    tables = []
    for v in range(n_var):
        off = (n_var - 1 - v) * TQ
        rel_chunk = kc - qc - v * (TQ // CHUNK)
        visible = (rel_chunk <= 0) & (rel_chunk >= -PREV_CHUNKS)
        tables.append(jnp.where(jnp.asarray(visible)[None], toep[:, :, off: off + W], NEG_INF))
    return jnp.stack(tables, axis=0)


def _attn(q, k, v, rel_bias):
    bn, s, _ = q.shape
    TQ = ATT_QROWS
    bias = _attn_bias_tables(rel_bias)
    n_var = bias.shape[0]
    tile = pl.BlockSpec((1, TQ, WIDTH), lambda b, t: (b, t, 0))
    whole = pl.BlockSpec((1, s, WIDTH), lambda b, t: (b, 0, 0))
    return pl.pallas_call(
        _attn_kernel,
        grid=(bn, s // TQ),
        in_specs=[tile, whole, whole,
                  pl.BlockSpec((1, N_HEADS, TQ, ATT_WINDOW),
                               lambda b, t: (jnp.minimum(t, n_var - 1), 0, 0, 0))],
        out_specs=tile,
        out_shape=jax.ShapeDtypeStruct((bn, s, WIDTH), f32),
        compiler_params=_cparams(("parallel", "arbitrary")),
        name="attn",
    )(q, k, v, bias)


def _store_tile_rows(ref, x):
    rows = x.shape[0]
    for j in range(ROW_TILES):
        ref[pl.ds(j, rows, stride=ROW_TILES), :] = x[:, j * 128:(j + 1) * 128]


def _load_tile_rows(ref, rows):
    return jnp.concatenate(
        [ref[pl.ds(j, rows, stride=ROW_TILES), :] for j in range(ROW_TILES)], axis=1)


def _route(logits, tri, counts):
    lane = lax.broadcasted_iota(jnp.int32, logits.shape, 1).astype(f32)
    big = float(ROUTE_LANES)
    is_c = lane < N_GROUPS
    cl = jnp.where(is_c, logits, NEG_INF)
    cm = jnp.max(cl, axis=-1, keepdims=True)
    grp = jnp.min(jnp.where(cl == cm, lane, big), axis=-1, keepdims=True)
    p_grp = 1.0 / jnp.sum(jnp.where(is_c, jnp.exp(cl - cm), 0.0), axis=-1, keepdims=True)
    lo = N_GROUPS + grp * EXPERTS_PER_GROUP
    in_g = (lane >= lo) & (lane < lo + EXPERTS_PER_GROUP)
    fl = jnp.where(in_g, logits, NEG_INF)
    f1 = jnp.max(fl, axis=-1, keepdims=True)
    i1 = jnp.min(jnp.where(in_g & (fl == f1), lane, big), axis=-1, keepdims=True)
    fl2 = jnp.where(lane == i1, NEG_INF, fl)
    in_g2 = in_g & (lane != i1)
    f2 = jnp.max(fl2, axis=-1, keepdims=True)
    i2 = jnp.min(jnp.where(in_g2 & (fl2 == f2), lane, big), axis=-1, keepdims=True)
    e2 = jnp.exp(f2 - f1)
    w1 = p_grp / (1.0 + e2)
    w2 = p_grp * e2 / (1.0 + e2)

    hot1 = (lane == i1).astype(f32)
    hot2 = (lane == i2).astype(f32)
    pre1 = jnp.dot(tri, hot1.astype(bf16), preferred_element_type=f32)
    pre2 = jnp.dot(tri, hot2.astype(bf16), preferred_element_type=f32)
    tot1 = jnp.sum(hot1, axis=0, keepdims=True)
    tot2 = jnp.sum(hot2, axis=0, keepdims=True)
    rank1 = jnp.sum(hot1 * (pre1 + counts), axis=-1, keepdims=True)
    rank2 = jnp.sum(hot2 * (pre2 + counts + tot1), axis=-1, keepdims=True)

    out = jnp.where(lane == 0, w1, 0.0)
    out = jnp.where(lane == 1, w2, out)
    out = jnp.where(lane == 2, i1 - N_GROUPS, out)
    out = jnp.where(lane == 3, i2 - N_GROUPS, out)
    out = jnp.where(lane == 4, rank1, out)
    out = jnp.where(lane == 5, rank2, out)
    return out, counts + tot1 + tot2


def _merge_kernel(yr_ref, ya_ref, gt_ref, x_ref, g1_ref, sh_ref, sc_ref, ng_ref, wbr_ref,
                  wba_ref, wout_ref, wrt_ref, brt_ref, tri_ref, x1_ref, h2_ref, route_ref,
                  cnt_ref):
    @pl.when((pl.program_id(0) == 0) & (pl.program_id(1) == 0))
    def _():
        cnt_ref[...] = jnp.zeros_like(cnt_ref)

    t1 = jnp.dot(yr_ref[0].astype(bf16), wbr_ref[...], preferred_element_type=f32)
    t2 = jnp.dot(ya_ref[0].astype(bf16), wba_ref[...], preferred_element_type=f32)
    gt = gt_ref[0]
    m = gt[:, :D_MODEL] * t1 + gt[:, D_MODEL:] * t2
    mixed = jnp.dot(m.astype(bf16), wout_ref[...], preferred_element_type=f32)
    x1 = x_ref[0] + g1_ref[0] * mixed
    x1_ref[0] = x1
    h2 = _rms_mod(x1, ng_ref[...], sh_ref[0], sc_ref[0])
    _store_tile_rows(h2_ref.at[0], h2)
    h2_hi = h2.astype(bf16)
    h2_lo = (h2 - h2_hi.astype(f32)).astype(bf16)
    pa = jnp.dot(h2_hi, wrt_ref[...], preferred_element_type=f32)
    pb = jnp.dot(h2_lo, wrt_ref[...], preferred_element_type=f32)
    logits = ((pa[:, :ROUTE_LANES] + pb[:, :ROUTE_LANES])
              + (pa[:, ROUTE_LANES:] + pb[:, ROUTE_LANES:]) + brt_ref[...])
    route, counts = _route(logits, tri_ref[...], cnt_ref[...])
    route_ref[0] = route
    cnt_ref[...] = counts


def _merge(y_r, y_a, gates, x, g1, sh2, sc2, norm2_g, w_br, w_ba, w_out, w_rt, b_rt):
    bn, s, d = x.shape
    tm = min(MERGE_ROWS, s)
    row = lambda b, i: (b, i, 0)
    per_b = lambda b, i: (b, 0, 0)
    const = lambda b, i: (0, 0)
    full = lambda t: pl.BlockSpec(t.shape, const)
    tri = jnp.asarray(np.tril(np.ones((tm, tm), np.float32), -1), bf16)
    return pl.pallas_call(
        _merge_kernel,
        grid=(bn, s // tm),
        in_specs=[pl.BlockSpec((1, tm, WIDTH), row),
                  pl.BlockSpec((1, tm, WIDTH), row),
                  pl.BlockSpec((1, tm, GATE_COLS), row),
                  pl.BlockSpec((1, tm, d), row),
                  pl.BlockSpec((1, 1, d), per_b),
                  pl.BlockSpec((1, 1, d), per_b),
                  pl.BlockSpec((1, 1, d), per_b),
                  pl.BlockSpec((1, d), const),
                  full(w_br), full(w_ba), full(w_out), full(w_rt), full(b_rt), full(tri)],
        out_specs=[pl.BlockSpec((1, tm, d), row),
                   pl.BlockSpec((1, tm * ROW_TILES, 128), row),
                   pl.BlockSpec((1, tm, ROUTE_LANES), row),
                   pl.BlockSpec((1, ROUTE_LANES), const)],
        out_shape=[jax.ShapeDtypeStruct((bn, s, d), f32),
                   jax.ShapeDtypeStruct((bn, s * ROW_TILES, 128), f32),
                   jax.ShapeDtypeStruct((bn, s, ROUTE_LANES), f32),
                   jax.ShapeDtypeStruct((1, ROUTE_LANES), f32)],
        compiler_params=_cparams(("arbitrary", "arbitrary")),
        name="merge",
    )(y_r, y_a, gates, x, g1.reshape(bn, 1, d), sh2.reshape(bn, 1, d), sc2.reshape(bn, 1, d),
      norm2_g.reshape(1, d), w_br, w_ba, w_out, w_rt, b_rt, tri)


def _row_copy(src, src_row, dst, dst_row, sem):
    return pltpu.make_async_copy(src.at[src_row], dst.at[pl.ds(dst_row * ROW_TILES, ROW_TILES)],
                                 sem)


def _moe_kernel(tok_ref, be_ref, j0_ref, h2_hbm, wg_ref, wu_ref, wd_ref, o_ref, xbuf, sem):
    del be_ref
    i = pl.program_id(0)
    last = pl.num_programs(0) - 1
    n_asg = tok_ref.shape[0]

    def issue(blk, slot):
        j0 = j0_ref[blk]
        for r in range(MOE_ROWS):
            tok = tok_ref[jnp.minimum(j0 + r, n_asg - 1)]
            _row_copy(h2_hbm, tok, xbuf.at[slot], r, sem.at[slot]).start()

    def wait(slot):
        for r in range(MOE_ROWS):
            _row_copy(h2_hbm, 0, xbuf.at[slot], r, sem.at[slot]).wait()

    for slot in range(2):
        @pl.when(i % 2 == slot)
        def _(slot=slot):
            if slot == 0:
                @pl.when(i == 0)
                def _():
                    issue(0, 0)

            wait(slot)
            issue(jnp.minimum(i + 1, last), 1 - slot)
            xb = _load_tile_rows(xbuf.at[slot], MOE_ROWS).astype(bf16)
            gate = jnp.dot(xb, wg_ref[0], preferred_element_type=f32)
            up = jnp.dot(xb, wu_ref[0], preferred_element_type=f32)
            hid = gate * _sigmoid(gate) * up
            _store_tile_rows(
                o_ref, jnp.dot(hid.astype(bf16), wd_ref[0], preferred_element_type=f32))

            @pl.when(i == last)
            def _():
                wait(1 - slot)


def _moe(h2_tiles, tok_sorted, block_e, block_j0, wg, wu, wd):
    d = D_MODEL
    n_blocks = block_e.shape[0]
    blk_rows = MOE_ROWS * ROW_TILES
    by_expert = lambda i, tok, be, j0: (be[i], 0, 0)
    grid_spec = pltpu.PrefetchScalarGridSpec(
        num_scalar_prefetch=3,
        grid=(n_blocks,),
        in_specs=[pl.BlockSpec(memory_space=pl.ANY),
                  pl.BlockSpec((1, d, D_EXPERT), by_expert),
                  pl.BlockSpec((1, d, D_EXPERT), by_expert),
                  pl.BlockSpec((1, D_EXPERT, d), by_expert)],
        out_specs=pl.BlockSpec((blk_rows, 128), lambda i, tok, be, j0: (i, 0)),
        scratch_shapes=[pltpu.VMEM((2, blk_rows, 128), f32), pltpu.SemaphoreType.DMA((2,))],
    )
    yb = pl.pallas_call(
        _moe_kernel,
        grid_spec=grid_spec,
        out_shape=jax.ShapeDtypeStruct((n_blocks * blk_rows, 128), f32),
        compiler_params=_cparams(("arbitrary",)),
        name="moe",
    )(tok_sorted, block_e, block_j0, h2_tiles, wg, wu, wd)
    return yb.reshape(n_blocks * MOE_ROWS, ROW_TILES, 128)


def _combine_kernel(d0_ref, d1_ref, yb_hbm, x1_ref, rt_ref, g2_ref, o_ref, buf, sem):
    tc = buf.shape[2] // ROW_TILES
    step = pl.program_id(0) * pl.num_programs(1) + pl.program_id(1)
    n_steps = pl.num_programs(0) * pl.num_programs(1)

    def issue(for_step, slot):
        base = for_step * tc
        for r in range(tc):
            _row_copy(yb_hbm, d0_ref[base + r], buf.at[slot, 0], r, sem.at[slot]).start()
            _row_copy(yb_hbm, d1_ref[base + r], buf.at[slot, 1], r, sem.at[slot]).start()

    def wait(slot):
        for r in range(tc):
            _row_copy(yb_hbm, 0, buf.at[slot, 0], r, sem.at[slot]).wait()
            _row_copy(yb_hbm, 0, buf.at[slot, 1], r, sem.at[slot]).wait()

    @pl.when(step == 0)
    def _():
        issue(step, 0)

    for slot in range(2):
        @pl.when(step % 2 == slot)
        def _(slot=slot):
            @pl.when(step + 1 < n_steps)
            def _():
                issue(step + 1, 1 - slot)

            wait(slot)
            rt = rt_ref[0]
            moe = (rt[:, 0:1] * _load_tile_rows(buf.at[slot, 0], tc)
                   + rt[:, 1:2] * _load_tile_rows(buf.at[slot, 1], tc))
            o_ref[0] = x1_ref[0] + g2_ref[0] * moe


def _combine(dest0, dest1, yb, x1, route, g2):
    bn, s, d = x1.shape
    tc = min(COMBINE_ROWS, s)
    row = lambda b, i, d0, d1: (b, i, 0)
    grid_spec = pltpu.PrefetchScalarGridSpec(
        num_scalar_prefetch=2,
        grid=(bn, s // tc),
        in_specs=[pl.BlockSpec(memory_space=pl.ANY),
                  pl.BlockSpec((1, tc, d), row),
                  pl.BlockSpec((1, tc, ROUTE_LANES), row),
                  pl.BlockSpec((1, 1, d), lambda b, i, d0, d1: (b, 0, 0))],
        out_specs=pl.BlockSpec((1, tc, d), row),
        scratch_shapes=[pltpu.VMEM((2, 2, tc * ROW_TILES, 128), f32),
                        pltpu.SemaphoreType.DMA((2,))],
    )
    return pl.pallas_call(
        _combine_kernel,
        grid_spec=grid_spec,
        out_shape=jax.ShapeDtypeStruct((bn, s, d), f32),
        compiler_params=_cparams(("arbitrary", "arbitrary")),
        name="combine",
    )(dest0, dest1, yb, x1, route, g2.reshape(bn, 1, d))


def _dispatch_plan(route, counts):
    n_tok = route.shape[0]
    counts = counts.astype(jnp.int32)
    padded = (counts + MOE_ROWS - 1) // MOE_ROWS * MOE_ROWS
    pad_ends = jnp.cumsum(padded)
    pad_starts = pad_ends - padded
    starts = jnp.cumsum(counts) - counts
    n_blocks = (2 * n_tok + N_EXPERTS * MOE_ROWS) // MOE_ROWS
    eid = route[:, 2:4].astype(jnp.int32)
    rank = route[:, 4:6].astype(jnp.int32)
    hot = eid[:, :, None] == jnp.arange(N_EXPERTS, dtype=jnp.int32)
    dest = rank + jnp.sum(jnp.where(hot, pad_starts, 0), axis=-1)
    tok = jnp.broadcast_to(jnp.arange(n_tok, dtype=jnp.int32)[:, None], (n_tok, 2))
    _, tok_sorted = lax.sort((dest.reshape(-1), tok.reshape(-1)), num_keys=1)
    block_row = jnp.arange(n_blocks, dtype=jnp.int32) * MOE_ROWS
    block_e = jnp.minimum(jnp.sum(pad_ends[None, :] <= block_row[:, None], axis=-1),
                          N_EXPERTS - 1).astype(jnp.int32)
    hot_b = block_e[:, None] == jnp.arange(N_EXPERTS, dtype=jnp.int32)
    block_j0 = block_row + jnp.sum(jnp.where(hot_b, starts - pad_starts, 0), axis=-1)
    block_j0 = jnp.clip(block_j0, 0, 2 * n_tok - 1).astype(jnp.int32)
    return tok_sorted, block_e, block_j0, dest[:, 0], dest[:, 1]


def kernel(x, c, w_ada, b_ada, norm1_g, w_in, rwkv_mu, rwkv_w0, rwkv_w_up, rwkv_a0, rwkv_a_up, rwkv_g_up, rwkv_k_k, rwkv_k_a, rwkv_r_k, rwkv_lnx_g, rwkv_lnx_b, attn_q_g, attn_k_g, attn_rel_bias, w_branch_rwkv, w_branch_attn, w_out, norm2_g, router_coarse_w, router_coarse_b, router_fine_w, router_fine_b, expert_w_gate, expert_w_up, expert_w_down):
    bn, s, d = x.shape
    depth = w_ada.shape[0]
    for l in range(depth):
        mod = _ada(c, w_ada[l], b_ada[l])
        sh1, sc1, g1, sh2, sc2, g2 = [mod[:, i * d:(i + 1) * d] for i in range(6)]

        p_rwkv, q, k, v, gates = _inproj(x, norm1_g[l], sh1, sc1, w_in[l].astype(bf16),
                                         attn_q_g[l], attn_k_g[l])
        y_r = _rwkv(p_rwkv, rwkv_mu[l], rwkv_w0[l], rwkv_w_up[l], rwkv_a0[l], rwkv_a_up[l],
                    rwkv_g_up[l], rwkv_k_k[l], rwkv_k_a[l], rwkv_r_k[l].reshape(-1),
                    rwkv_lnx_g[l], rwkv_lnx_b[l])
        y_a = _attn(q, k, v, attn_rel_bias[l])

        n_rt = N_GROUPS + N_EXPERTS
        w_rt = jnp.zeros((d, ROUTE_LANES), f32)
        w_rt = w_rt.at[:, :N_GROUPS].set(router_coarse_w[l]).at[:, N_GROUPS:n_rt].set(router_fine_w[l])
        w_rt_hi = w_rt.astype(bf16)
        w_rt = jnp.concatenate([w_rt_hi, (w_rt - w_rt_hi.astype(f32)).astype(bf16)], axis=1)
        b_rt = jnp.zeros((1, ROUTE_LANES), f32)
        b_rt = b_rt.at[0, :N_GROUPS].set(router_coarse_b[l]).at[0, N_GROUPS:n_rt].set(router_fine_b[l])
        x1, h2, route, counts = _merge(y_r, y_a, gates, x, g1, sh2, sc2, norm2_g[l],
                                       w_branch_rwkv[l].astype(bf16),
                                       w_branch_attn[l].astype(bf16),
                                       w_out[l].astype(bf16), w_rt, b_rt)

        tok_sorted, block_e, block_j0, dest0, dest1 = _dispatch_plan(
            route.reshape(bn * s, ROUTE_LANES), counts[0, N_GROUPS:n_rt])
        yb = _moe(h2.reshape(bn * s, ROW_TILES, 128), tok_sorted, block_e, block_j0,
                  expert_w_gate[l].astype(bf16), expert_w_up[l].astype(bf16),
                  expert_w_down[l].astype(bf16))
        x = _combine(dest0, dest1, yb, x1, route, g2)
    return x
```

```python
import functools

import numpy as np
import jax
import jax.numpy as jnp
from jax import lax
from jax.experimental import pallas as pl
from jax.experimental.pallas import tpu as pltpu

f32 = jnp.float32
bf16 = jnp.bfloat16
HI = lax.Precision.HIGHEST

D_MODEL = 1024
HEAD_DIM = 64
N_HEADS = 8
WIDTH = N_HEADS * HEAD_DIM
DECAY_LORA = 64
ICLR_LORA = 64
GATE_LORA = 128
RWKV_COLS = 3 * WIDTH + DECAY_LORA + ICLR_LORA + GATE_LORA
ATT_COLS = 3 * WIDTH
GATE_COLS = 2 * D_MODEL
CHUNK = 64
PREV_CHUNKS = 8
REL_CLIP = 128
N_GROUPS = 4
EXPERTS_PER_GROUP = 8
N_EXPERTS = N_GROUPS * EXPERTS_PER_GROUP
D_EXPERT = 512
NORM_EPS = 1e-6
GN_EPS = 64e-5
NEG_INF = -1e30

INPROJ_ROWS = 256
RWKV_CHUNK = 64
RWKV_BATCH_ROWS = 4
ATT_QROWS = 128
ATT_WINDOW = PREV_CHUNKS * CHUNK + ATT_QROWS
MERGE_ROWS = 512
MERGE_PART = 512
MOE_ROWS = 512
COMBINE_ROWS = 128
ROUTE_LANES = 128
ROW_TILES = D_MODEL // 128
VMEM_LIMIT = 56 * 1024 * 1024


def _cparams(sem):
    return pltpu.CompilerParams(dimension_semantics=sem, vmem_limit_bytes=VMEM_LIMIT)


def _sigmoid(x):
    return 1.0 / (1.0 + jnp.exp(-x))


def _ada_kernel(c_ref, w_ref, b_ref, o_ref):
    c = c_ref[...]
    s = c * _sigmoid(c)
    o_ref[...] = jnp.dot(s, w_ref[...], precision=HI, preferred_element_type=f32) + b_ref[...]


def _ada(c, w, b):
    bn, d = c.shape
    n = w.shape[1]
    tn = 1024
    return pl.pallas_call(
        _ada_kernel,
        grid=(n // tn,),
        in_specs=[pl.BlockSpec((bn, d), lambda j: (0, 0)),
                  pl.BlockSpec((d, tn), lambda j: (0, j)),
                  pl.BlockSpec((1, tn), lambda j: (0, j))],
        out_specs=pl.BlockSpec((bn, tn), lambda j: (0, j)),
        out_shape=jax.ShapeDtypeStruct((bn, n), f32),
        compiler_params=_cparams(("arbitrary",)),
        name="ada",
    )(c, w, b.reshape(1, n))


def _rms_mod(x, gain, shift, scale):
    ms = jnp.mean(x * x, axis=-1, keepdims=True)
    h = x * lax.rsqrt(ms + NORM_EPS) * gain
    return h * (1.0 + scale) + shift


def _head_norm(x, gain, ones_bd):
    outs = []
    for c in range(WIDTH // 128):
        xc = x[:, c * 128:(c + 1) * 128]
        ms = jnp.dot((xc * xc).astype(bf16), ones_bd, preferred_element_type=f32) * (1.0 / HEAD_DIM)
        outs.append(xc * lax.rsqrt(ms + NORM_EPS))
    return jnp.concatenate(outs, axis=1) * gain


def _inproj_kernel(x_ref, g_ref, sh_ref, sc_ref, w_ref, qg_ref, kg_ref, bd_ref,
                   pr_ref, q_ref, k_ref, v_ref, pg_ref):
    h = _rms_mod(x_ref[0], g_ref[...], sh_ref[0], sc_ref[0]).astype(bf16)
    c1, c2 = RWKV_COLS, RWKV_COLS + ATT_COLS
    pr_ref[0] = jnp.dot(h, w_ref[:, :c1], preferred_element_type=f32)
    pa = jnp.dot(h, w_ref[:, c1:c2], preferred_element_type=f32)
    ones_bd = bd_ref[...]
    q_ref[0] = _head_norm(pa[:, :WIDTH], qg_ref[...], ones_bd).astype(bf16)
    k_ref[0] = _head_norm(pa[:, WIDTH:2 * WIDTH], kg_ref[...], ones_bd).astype(bf16)
    v_ref[0] = pa[:, 2 * WIDTH:].astype(bf16)
    pg_ref[0] = _sigmoid(jnp.dot(h, w_ref[:, c2:], preferred_element_type=f32))


def _inproj(x, gain, shift, scale, w_bf, q_g, k_g):
    bn, s, d = x.shape
    tm = min(INPROJ_ROWS, s)
    n = w_bf.shape[1]
    row = lambda b, i: (b, i, 0)
    per_b = lambda b, i: (b, 0, 0)
    const = lambda b, i: (0, 0)
    per_head = lambda g, scale: (jnp.tile(g.astype(f32), N_HEADS) * scale).reshape(1, WIDTH)
    lane_head = np.arange(128) // HEAD_DIM
    ones_bd = jnp.asarray(lane_head[:, None] == lane_head[None, :], bf16)
    return pl.pallas_call(
        _inproj_kernel,
        grid=(bn, s // tm),
        in_specs=[pl.BlockSpec((1, tm, d), row),
                  pl.BlockSpec((1, d), const),
                  pl.BlockSpec((1, 1, d), per_b),
                  pl.BlockSpec((1, 1, d), per_b),
                  pl.BlockSpec((d, n), const),
                  pl.BlockSpec((1, WIDTH), const),
                  pl.BlockSpec((1, WIDTH), const),
                  pl.BlockSpec((128, 128), const)],
        out_specs=[pl.BlockSpec((1, tm, RWKV_COLS), row),
                   pl.BlockSpec((1, tm, WIDTH), row),
                   pl.BlockSpec((1, tm, WIDTH), row),
                   pl.BlockSpec((1, tm, WIDTH), row),
                   pl.BlockSpec((1, tm, GATE_COLS), row)],
        out_shape=[jax.ShapeDtypeStruct((bn, s, RWKV_COLS), f32),
                   jax.ShapeDtypeStruct((bn, s, WIDTH), bf16),
                   jax.ShapeDtypeStruct((bn, s, WIDTH), bf16),
                   jax.ShapeDtypeStruct((bn, s, WIDTH), bf16),
                   jax.ShapeDtypeStruct((bn, s, GATE_COLS), f32)],
        compiler_params=_cparams(("parallel", "parallel")),
        name="inproj",
    )(x, gain.reshape(1, d), shift.reshape(bn, 1, d), scale.reshape(bn, 1, d), w_bf,
      per_head(q_g, HEAD_DIM ** -0.5), per_head(k_g, 1.0), ones_bd)


def _split_bf16(x):
    hi = x.astype(bf16)
    return hi, (x - hi.astype(f32)).astype(bf16)


def _mm_split(x, w2):
    rows, n = x.shape[0], w2.shape[1] // 2
    pr = jnp.dot(jnp.concatenate(_split_bf16(x), axis=0), w2, preferred_element_type=f32)
    return (pr[:rows, :n] + pr[rows:, :n]) + (pr[:rows, n:] + pr[rows:, n:])


def _bmm(a, b):
    return jnp.dot(a, b, preferred_element_type=f32)


def _bmm_nt(a, b):
    return lax.dot_general(a, b, (((1,), (1,)), ((), ())), preferred_element_type=f32)


def _bmm_tn(a, b):
    return lax.dot_general(a, b, (((0,), (0,)), ((), ())), preferred_element_type=f32)


def _rwkv_kernel(p_ref, mu_ref, w0_ref, wup_ref, a0_ref, aup_ref, gup_ref, kk_ref, ka_ref,
                 rk_ref, lg_ref, lb_ref, o_ref, last_ref, st_ref):
    C = RWKV_CHUNK
    c_idx = pl.program_id(1)

    @pl.when(c_idx == 0)
    def _():
        last_ref[...] = jnp.zeros_like(last_ref)
        st_ref[...] = jnp.zeros_like(st_ref)

    NB = p_ref.shape[0]
    R = NB * C
    rows_of = [slice(b * C, (b + 1) * C) for b in range(NB)]
    pc = p_ref[...].reshape(R, RWKV_COLS)
    row_full = lax.broadcasted_iota(jnp.int32, pc.shape, 0)
    prev = pltpu.roll(pc, 1, axis=0)
    for b in range(NB):
        prev = jnp.where(row_full == b * C, last_ref[b:b + 1, :], prev)
        last_ref[b:b + 1, :] = pc[(b + 1) * C - 1:(b + 1) * C, :]
    ps = pc + (prev - pc) * mu_ref[...]

    o1, o2, o3 = WIDTH, 2 * WIDTH, 3 * WIDTH
    o4, o5 = o3 + DECAY_LORA, o3 + DECAY_LORA + ICLR_LORA
    r, k, v = ps[:, :o1], ps[:, o1:o2], ps[:, o2:o3]
    wd, ad, gd = ps[:, o3:o4], ps[:, o4:o5], ps[:, o5:]

    wraw = w0_ref[...] + _mm_split(jnp.tanh(wd), wup_ref[...])
    z = -wraw
    softplus = jnp.maximum(z, 0.0) + jnp.log(1.0 + jnp.exp(-jnp.abs(z)))
    logdec = -jnp.exp(-softplus - 0.5)
    a = _sigmoid(a0_ref[...] + _mm_split(ad, aup_ref[...]))
    g = _mm_split(_sigmoid(gd), gup_ref[...])
    kkr = k * kk_ref[...]
    k2 = k * (1.0 + (a - 1.0) * ka_ref[...])
    rkk = r * k2 * rk_ref[...]

    ri = lax.broadcasted_iota(jnp.int32, (R, R), 0)
    ci = lax.broadcasted_iota(jnp.int32, (R, R), 1)
    tri = ((ri >= ci) & (ri // C == ci // C)).astype(bf16)
    ld_hi = logdec.astype(bf16)
    ld_r = logdec - ld_hi.astype(f32)
    ld_mid = ld_r.astype(bf16)
    ld_lo = (ld_r - ld_mid.astype(f32)).astype(bf16)
    cum3 = _bmm(tri, jnp.concatenate([ld_hi, ld_mid, ld_lo], axis=1))
    cum = cum3[:, :WIDTH] + cum3[:, WIDTH:2 * WIDTH] + cum3[:, 2 * WIDTH:]
    e_pos = jnp.exp(cum)
    e_prev = jnp.exp(cum - logdec)
    e_neg = jnp.exp(-cum)
    cum_ends = [cum[(b + 1) * C - 1:(b + 1) * C, :] for b in range(NB)]
    e_end = jnp.exp(jnp.concatenate([jnp.broadcast_to(ce, (C, WIDTH)) for ce in cum_ends], axis=0)
                    - cum)
    g_end = [jnp.exp(ce) for ce in cum_ends]

    C2 = 2 * C
    lane1 = lax.broadcasted_iota(jnp.int32, (1, 128), 1)
    keep_lo = (lane1 < HEAD_DIM).astype(bf16)
    keep_hi = (lane1 >= HEAD_DIM).astype(bf16)

    def bd(x):
        return jnp.concatenate([x * keep_lo, x * keep_hi], axis=0)

    r2 = lax.broadcasted_iota(jnp.int32, (C2, C2), 0)
    c2 = lax.broadcasted_iota(jnp.int32, (C2, C2), 1)
    ones_bd = ((r2 // HEAD_DIM) == (c2 // HEAD_DIM)).astype(bf16)
    eye = (r2 == c2).astype(f32)
    rr, cc = r2 % C, c2 % C
    strict = rr > cc
    blk16 = (rr // 16) == (cc // 16)
    blk32 = (rr // 32) == (cc // 32)
    m_diag16 = strict & blk16
    m_off32 = strict & blk32 & jnp.logical_not(blk16)
    m_off64 = strict & jnp.logical_not(blk32)
    r4 = lax.broadcasted_iota(jnp.int32, (C2, 2 * C2), 0) % C
    c4 = lax.broadcasted_iota(jnp.int32, (C2, 2 * C2), 1) % C
    strict2 = r4 > c4
    incl2 = r4 >= c4
    zeros_bd = jnp.zeros((C2, C2), bf16)

    P = range(N_HEADS // 2)
    ps_ = [slice(p * 128, (p + 1) * 128) for p in P]

    def head_sums(x):
        return jnp.concatenate([_bmm(x[:, s], ones_bd) for s in ps_], axis=1)

    kkn = kkr * lax.rsqrt(jnp.maximum(head_sums((kkr * kkr).astype(bf16)), 1e-24))
    bh = kkn * a
    at_w = (-kkn * e_prev).astype(bf16)
    rt_w = (r * e_pos).astype(bf16)
    bt_w = (bh * e_neg).astype(bf16)
    kt_w = (k2 * e_neg).astype(bf16)
    bhat_w = (bh * e_end).astype(bf16)
    khat_w = (k2 * e_end).astype(bf16)
    v_w = v.astype(bf16)

    units = [(b, p) for b in range(NB) for p in P]
    P = range(len(units))

    def sub(x, q):
        return x[rows_of[units[q][0]], ps_[units[q][1]]]

    at = [bd(sub(at_w, q)) for q in P]
    rt = [bd(sub(rt_w, q)) for q in P]
    vb = [bd(sub(v_w, q)) for q in P]
    smat = [_bmm_nt(jnp.concatenate([at[p], rt[p]], axis=0),
                    jnp.concatenate([bd(sub(bt_w, p)), bd(sub(kt_w, p))], axis=0)) for p in P]
    n_aak = [jnp.where(strict2, smat[p][:C2], 0.0) for p in P]
    arb_ark = [jnp.where(incl2, smat[p][C2:], 0.0).astype(bf16) for p in P]
    nmat = [n_aak[p][:, :C2] for p in P]

    pw = [jnp.where(m_diag16, nmat[p], 0.0) for p in P]
    tinv = [eye + pw[p] for p in P]
    for _ in range(3):
        pwb = [pw[p].astype(bf16) for p in P]
        pw = [_bmm(pwb[p], pwb[p]) for p in P]
        tinv = [tinv[p] + _bmm(tinv[p].astype(bf16), pw[p].astype(bf16)) for p in P]
    for m_off in (m_off32, m_off64):
        tb = [tinv[p].astype(bf16) for p in P]
        mid = [_bmm(jnp.where(m_off, nmat[p], 0.0).astype(bf16), tb[p]).astype(bf16) for p in P]
        tinv = [tinv[p] + _bmm(tb[p], mid[p]) for p in P]
    tb = [tinv[p].astype(bf16) for p in P]

    aakv = [_bmm(n_aak[p][:, C2:].astype(bf16), vb[p]).astype(bf16) for p in P]
    m1u0 = [_bmm(tb[p], jnp.concatenate([at[p], aakv[p]], axis=1)).astype(bf16) for p in P]
    rhs = [jnp.concatenate([m1u0[p], jnp.concatenate([zeros_bd, vb[p]], axis=1)], axis=0)
           for p in P]
    my = [_bmm(arb_ark[p], rhs[p]) for p in P]
    gd_ = [_bmm_tn(jnp.concatenate([bd(sub(bhat_w, p)), bd(sub(khat_w, p))], axis=0), rhs[p])
           for p in P]
    m2 = [(rt[p].astype(f32) + my[p][:, :C2]).astype(bf16) for p in P]
    gmat = [(eye * g_end[units[p][0]][:, ps_[units[p][1]]] + gd_[p][:, :C2]).astype(bf16)
            for p in P]
    upd = [_bmm(jnp.concatenate([m2[p], gmat[p]], axis=0), st_ref[p].astype(bf16)) for p in P]

    lane_c = lax.broadcasted_iota(jnp.int32, (C, 128), 1)
    ys = []
    for p in P:
        st_ref[p] = upd[p][C2:] + gd_[p][:, C2:]
        y_bd = upd[p][:C2] + my[p][:, C2:]
        ys.append(jnp.where(lane_c < HEAD_DIM, y_bd[:C], y_bd[C:]))
    n_pairs = len(ps_)
    y = jnp.concatenate([jnp.concatenate(ys[b * n_pairs:(b + 1) * n_pairs], axis=1)
                         for b in range(NB)], axis=0)

    y_hi = y.astype(bf16)
    y_lo = (y - y_hi.astype(f32)).astype(bf16)
    sums = head_sums(jnp.concatenate([y_hi, y_lo, rkk.astype(bf16)], axis=0))
    yc = y - (sums[:R] + sums[R:2 * R]) * (1.0 / HEAD_DIM)
    var = head_sums((yc * yc).astype(bf16)) * (1.0 / HEAD_DIM)
    yn = yc * lax.rsqrt(var + GN_EPS) * lg_ref[...] + lb_ref[...]
    o_ref[...] = ((yn + sums[2 * R:] * v) * g).reshape(NB, C, WIDTH)


def _rwkv(p_rwkv, mu, w0, w_up, a0, a_up, g_up, k_k, k_a, r_k, lnx_g, lnx_b):
    bn, s, _ = p_rwkv.shape
    C = RWKV_CHUNK
    vec = lambda t: t.reshape(1, -1)
    const = lambda b, c: (0, 0)
    full = lambda t: pl.BlockSpec(t.shape, const)
    hi_lo = lambda w: jnp.concatenate(_split_bf16(w.astype(f32)), axis=1)
    args = [vec(mu), vec(w0), hi_lo(w_up), vec(a0), hi_lo(a_up), hi_lo(g_up), vec(k_k), vec(k_a),
            vec(r_k), vec(lnx_g), vec(lnx_b)]
    nb = RWKV_BATCH_ROWS if bn % RWKV_BATCH_ROWS == 0 else 1
    return pl.pallas_call(
        _rwkv_kernel,
        grid=(bn // nb, s // C),
        in_specs=[pl.BlockSpec((nb, C, RWKV_COLS), lambda b, c: (b, c, 0))] + [full(t) for t in args],
        out_specs=pl.BlockSpec((nb, C, WIDTH), lambda b, c: (b, c, 0)),
        out_shape=jax.ShapeDtypeStruct((bn, s, WIDTH), f32),
        scratch_shapes=[pltpu.VMEM((nb, RWKV_COLS), f32),
                        pltpu.VMEM((nb * N_HEADS // 2, 2 * HEAD_DIM, 2 * HEAD_DIM), f32)],
        compiler_params=_cparams(("parallel", "arbitrary")),
        name="rwkv",
    )(p_rwkv, *args)


def _attn_kernel(q_ref, k_ref, v_ref, bias_ref, o_ref):
    TQ = ATT_QROWS
    t = pl.program_id(1)
    g0 = pl.multiple_of(jnp.maximum(t - PREV_CHUNKS * CHUNK // TQ, 0) * TQ, TQ)
    q = q_ref[0]
    lane = lax.broadcasted_iota(jnp.int32, (1, 128), 1)
    keep_lo = (lane < HEAD_DIM).astype(bf16)
    keep_hi = (lane >= HEAD_DIM).astype(bf16)
    lane_f = lax.broadcasted_iota(jnp.int32, (TQ, 128), 1)
    pairs = range(N_HEADS // 2)

    scores = []
    for p in pairs:
        qp = q[:, p * 128:(p + 1) * 128]
        qs = jnp.concatenate([qp * keep_lo, qp * keep_hi], axis=0)
        kp = k_ref[0, pl.ds(g0, ATT_WINDOW), p * 128:(p + 1) * 128]
        s = lax.dot_general(qs, kp, (((1,), (1,)), ((), ())), preferred_element_type=f32)
        scores.append(s + jnp.concatenate([bias_ref[0, 2 * p], bias_ref[0, 2 * p + 1]], axis=0))
    probs, denoms = [], []
    for p in pairs:
        m = jnp.max(scores[p], axis=-1, keepdims=True)
        e = jnp.exp(scores[p] - m)
        denoms.append(jnp.sum(e, axis=-1, keepdims=True))
        probs.append(e.astype(bf16))
    outs = []
    for p in pairs:
        vp = v_ref[0, pl.ds(g0, ATT_WINDOW), p * 128:(p + 1) * 128]
        o2 = jnp.dot(probs[p], vp, preferred_element_type=f32) / denoms[p]
        outs.append(jnp.where(lane_f < HEAD_DIM, o2[:TQ], o2[TQ:]))
    o_ref[0] = jnp.concatenate(outs, axis=-1)


def _attn_bias_tables(rel_bias):
    TQ, W = ATT_QROWS, ATT_WINDOW
    n_var = PREV_CHUNKS * CHUNK // TQ + 1
    n_h = rel_bias.shape[0]
    rb = rel_bias.astype(f32)
    ext = jnp.concatenate([jnp.broadcast_to(rb[:, :1], (n_h, W - REL_CLIP)), rb,
                           jnp.broadcast_to(rb[:, -1:], (n_h, W - REL_CLIP - 1))], axis=1)
    width = W + (n_var - 1) * TQ
    toep = jnp.stack([ext[:, TQ - qi: TQ - qi + width] for qi in range(TQ)], axis=1)
    qc = np.arange(TQ)[:, None] // CHUNK
    kc = np.arange(W)[None, :] // CHUNK
    tables = []
    for v in range(n_var):
        off = (n_var - 1 - v) * TQ
        rel_chunk = kc - qc - v * (TQ // CHUNK)
        visible = (rel_chunk <= 0) & (rel_chunk >= -PREV_CHUNKS)
        tables.append(jnp.where(jnp.asarray(visible)[None], toep[:, :, off: off + W], NEG_INF))
    return jnp.stack(tables, axis=0)


def _attn(q, k, v, rel_bias):
    bn, s, _ = q.shape
    TQ = ATT_QROWS
    bias = _attn_bias_tables(rel_bias)
    n_var = bias.shape[0]
    tile = pl.BlockSpec((1, TQ, WIDTH), lambda b, t: (b, t, 0))
    whole = pl.BlockSpec((1, s, WIDTH), lambda b, t: (b, 0, 0))
    return pl.pallas_call(
        _attn_kernel,
        grid=(bn, s // TQ),
        in_specs=[tile, whole, whole,
                  pl.BlockSpec((1, N_HEADS, TQ, ATT_WINDOW),
                               lambda b, t: (jnp.minimum(t, n_var - 1), 0, 0, 0))],
        out_specs=tile,
        out_shape=jax.ShapeDtypeStruct((bn, s, WIDTH), f32),
        compiler_params=_cparams(("parallel", "arbitrary")),
        name="attn",
    )(q, k, v, bias)


def _store_tile_rows(ref, x):
    rows = x.shape[0]
    for j in range(ROW_TILES):
        ref[pl.ds(j, rows, stride=ROW_TILES), :] = x[:, j * 128:(j + 1) * 128]


def _load_tile_rows(ref, rows):
    return jnp.concatenate(
        [ref[pl.ds(j, rows, stride=ROW_TILES), :] for j in range(ROW_TILES)], axis=1)


def _route(logits, tri, counts):
    lane = lax.broadcasted_iota(jnp.int32, logits.shape, 1).astype(f32)
    big = float(ROUTE_LANES)
    is_c = lane < N_GROUPS
    cl = jnp.where(is_c, logits, NEG_INF)
    cm = jnp.max(cl, axis=-1, keepdims=True)
    grp = jnp.min(jnp.where(cl == cm, lane, big), axis=-1, keepdims=True)
    p_grp = 1.0 / jnp.sum(jnp.where(is_c, jnp.exp(cl - cm), 0.0), axis=-1, keepdims=True)
    lo = N_GROUPS + grp * EXPERTS_PER_GROUP
    in_g = (lane >= lo) & (lane < lo + EXPERTS_PER_GROUP)
    fl = jnp.where(in_g, logits, NEG_INF)
    f1 = jnp.max(fl, axis=-1, keepdims=True)
    i1 = jnp.min(jnp.where(in_g & (fl == f1), lane, big), axis=-1, keepdims=True)
    fl2 = jnp.where(lane == i1, NEG_INF, fl)
    in_g2 = in_g & (lane != i1)
    f2 = jnp.max(fl2, axis=-1, keepdims=True)
    i2 = jnp.min(jnp.where(in_g2 & (fl2 == f2), lane, big), axis=-1, keepdims=True)
    e2 = jnp.exp(f2 - f1)
    w1 = p_grp / (1.0 + e2)
    w2 = p_grp * e2 / (1.0 + e2)

    hot1 = (lane == i1).astype(f32)
    hot2 = (lane == i2).astype(f32)
    pre1 = jnp.dot(tri, hot1.astype(bf16), preferred_element_type=f32)
    pre2 = jnp.dot(tri, hot2.astype(bf16), preferred_element_type=f32)
    tot1 = jnp.sum(hot1, axis=0, keepdims=True)
    tot2 = jnp.sum(hot2, axis=0, keepdims=True)
    rank1 = jnp.sum(hot1 * (pre1 + counts), axis=-1, keepdims=True)
    rank2 = jnp.sum(hot2 * (pre2 + counts + tot1), axis=-1, keepdims=True)

    out = jnp.where(lane == 0, w1, 0.0)
    out = jnp.where(lane == 1, w2, out)
    out = jnp.where(lane == 2, i1 - N_GROUPS, out)
    out = jnp.where(lane == 3, i2 - N_GROUPS, out)
    out = jnp.where(lane == 4, rank1, out)
    out = jnp.where(lane == 5, rank2, out)
    return out, counts + tot1 + tot2


def _merge_kernel(yr_ref, ya_ref, gt_ref, x_ref, g1_ref, sh_ref, sc_ref, ng_ref, wbr_ref,
                  wba_ref, wout_ref, wrt_ref, brt_ref, tri_ref, x1_ref, h2_ref, route_ref,
                  cnt_ref):
    @pl.when((pl.program_id(0) == 0) & (pl.program_id(1) == 0))
    def _():
        cnt_ref[...] = jnp.zeros_like(cnt_ref)

    part = tri_ref.shape[0]
    counts = cnt_ref[...]
    for i in range(x_ref.shape[1] // part):
        rs = pl.ds(i * part, part)
        t1 = jnp.dot(yr_ref[0, rs, :].astype(bf16), wbr_ref[...], preferred_element_type=f32)
        t2 = jnp.dot(ya_ref[0, rs, :].astype(bf16), wba_ref[...], preferred_element_type=f32)
        gt = gt_ref[0, rs, :]
        m = gt[:, :D_MODEL] * t1 + gt[:, D_MODEL:] * t2
        mixed = jnp.dot(m.astype(bf16), wout_ref[...], preferred_element_type=f32)
        x1 = x_ref[0, rs, :] + g1_ref[0] * mixed
        x1_ref[0, rs, :] = x1
        h2 = _rms_mod(x1, ng_ref[...], sh_ref[0], sc_ref[0])
        _store_tile_rows(h2_ref.at[0, pl.ds(i * part * ROW_TILES, part * ROW_TILES)], h2)
        h2_hi = h2.astype(bf16)
        h2_lo = (h2 - h2_hi.astype(f32)).astype(bf16)
        pa = jnp.dot(h2_hi, wrt_ref[...], preferred_element_type=f32)
        pb = jnp.dot(h2_lo, wrt_ref[...], preferred_element_type=f32)
        logits = ((pa[:, :ROUTE_LANES] + pb[:, :ROUTE_LANES])
                  + (pa[:, ROUTE_LANES:] + pb[:, ROUTE_LANES:]) + brt_ref[...])
        route, counts = _route(logits, tri_ref[...], counts)
        route_ref[0, rs, :] = route
    cnt_ref[...] = counts


def _merge(y_r, y_a, gates, x, g1, sh2, sc2, norm2_g, w_br, w_ba, w_out, w_rt, b_rt):
    bn, s, d = x.shape
    tm = min(MERGE_ROWS, s)
    row = lambda b, i: (b, i, 0)
    per_b = lambda b, i: (b, 0, 0)
    const = lambda b, i: (0, 0)
    full = lambda t: pl.BlockSpec(t.shape, const)
    part = min(MERGE_PART, tm)
    tri = jnp.asarray(np.tril(np.ones((part, part), np.float32), -1), bf16)
    return pl.pallas_call(
        _merge_kernel,
        grid=(bn, s // tm),
        in_specs=[pl.BlockSpec((1, tm, WIDTH), row),
                  pl.BlockSpec((1, tm, WIDTH), row),
                  pl.BlockSpec((1, tm, GATE_COLS), row),
                  pl.BlockSpec((1, tm, d), row),
                  pl.BlockSpec((1, 1, d), per_b),
                  pl.BlockSpec((1, 1, d), per_b),
                  pl.BlockSpec((1, 1, d), per_b),
                  pl.BlockSpec((1, d), const),
                  full(w_br), full(w_ba), full(w_out), full(w_rt), full(b_rt), full(tri)],
        out_specs=[pl.BlockSpec((1, tm, d), row),
                   pl.BlockSpec((1, tm * ROW_TILES, 128), row),
                   pl.BlockSpec((1, tm, ROUTE_LANES), row),
                   pl.BlockSpec((1, ROUTE_LANES), const)],
        out_shape=[jax.ShapeDtypeStruct((bn, s, d), f32),
                   jax.ShapeDtypeStruct((bn, s * ROW_TILES, 128), f32),
                   jax.ShapeDtypeStruct((bn, s, ROUTE_LANES), f32),
                   jax.ShapeDtypeStruct((1, ROUTE_LANES), f32)],
        compiler_params=_cparams(("arbitrary", "arbitrary")),
        name="merge",
    )(y_r, y_a, gates, x, g1.reshape(bn, 1, d), sh2.reshape(bn, 1, d), sc2.reshape(bn, 1, d),
      norm2_g.reshape(1, d), w_br, w_ba, w_out, w_rt, b_rt, tri)


def _row_copy(src, src_row, dst, dst_row, sem):
    return pltpu.make_async_copy(src.at[src_row], dst.at[pl.ds(dst_row * ROW_TILES, ROW_TILES)],
                                 sem)


def _moe_kernel(tok_ref, be_ref, j0_ref, h2_hbm, wg_ref, wu_ref, wd_ref, o_ref, xbuf, sem):
    del be_ref
    i = pl.program_id(0)
    last = pl.num_programs(0) - 1

    def issue(blk, slot):
        j0 = j0_ref[blk]
        for r in range(MOE_ROWS):
            _row_copy(h2_hbm, tok_ref[j0 + r], xbuf.at[slot], r, sem.at[slot]).start()

    def wait(slot):
        for r in range(MOE_ROWS):
            _row_copy(h2_hbm, 0, xbuf.at[slot], r, sem.at[slot]).wait()

    for slot in range(2):
        @pl.when(i % 2 == slot)
        def _(slot=slot):
            if slot == 0:
                @pl.when(i == 0)
                def _():
                    issue(0, 0)

            wait(slot)
            issue(jnp.minimum(i + 1, last), 1 - slot)
            xb = _load_tile_rows(xbuf.at[slot], MOE_ROWS).astype(bf16)
            gate = jnp.dot(xb, wg_ref[0].astype(bf16), preferred_element_type=f32)
            up = jnp.dot(xb, wu_ref[0].astype(bf16), preferred_element_type=f32)
            hid = gate * _sigmoid(gate) * up
            _store_tile_rows(
                o_ref, jnp.dot(hid.astype(bf16), wd_ref[0].astype(bf16),
                               preferred_element_type=f32))

            @pl.when(i == last)
            def _():
                wait(1 - slot)


def _moe(h2_tiles, tok_sorted, block_e, block_j0, wg, wu, wd):
    d = D_MODEL
    n_blocks = block_e.shape[0]
    blk_rows = MOE_ROWS * ROW_TILES
    by_expert = lambda i, tok, be, j0: (be[i], 0, 0)
    grid_spec = pltpu.PrefetchScalarGridSpec(
        num_scalar_prefetch=3,
        grid=(n_blocks,),
        in_specs=[pl.BlockSpec(memory_space=pl.ANY),
                  pl.BlockSpec((1, d, D_EXPERT), by_expert),
                  pl.BlockSpec((1, d, D_EXPERT), by_expert),
                  pl.BlockSpec((1, D_EXPERT, d), by_expert)],
        out_specs=pl.BlockSpec((blk_rows, 128), lambda i, tok, be, j0: (i, 0)),
        scratch_shapes=[pltpu.VMEM((2, blk_rows, 128), f32), pltpu.SemaphoreType.DMA((2,))],
    )
    yb = pl.pallas_call(
        _moe_kernel,
        grid_spec=grid_spec,
        out_shape=jax.ShapeDtypeStruct((n_blocks * blk_rows, 128), f32),
        compiler_params=_cparams(("arbitrary",)),
        name="moe",
    )(tok_sorted, block_e, block_j0, h2_tiles, wg, wu, wd)
    return yb.reshape(n_blocks * MOE_ROWS, ROW_TILES, 128)


def _combine_kernel(d0_ref, d1_ref, yb_hbm, x1_ref, rt_ref, g2_ref, o_ref, buf, sem):
    tc = buf.shape[2] // ROW_TILES
    step = pl.program_id(0) * pl.num_programs(1) + pl.program_id(1)
    n_steps = pl.num_programs(0) * pl.num_programs(1)

    def issue(for_step, slot):
        base = for_step * tc
        for r in range(tc):
            _row_copy(yb_hbm, d0_ref[base + r], buf.at[slot, 0], r, sem.at[slot]).start()
            _row_copy(yb_hbm, d1_ref[base + r], buf.at[slot, 1], r, sem.at[slot]).start()

    def wait(slot):
        for r in range(tc):
            _row_copy(yb_hbm, 0, buf.at[slot, 0], r, sem.at[slot]).wait()
            _row_copy(yb_hbm, 0, buf.at[slot, 1], r, sem.at[slot]).wait()

    @pl.when(step == 0)
    def _():
        issue(step, 0)

    for slot in range(2):
        @pl.when(step % 2 == slot)
        def _(slot=slot):
            @pl.when(step + 1 < n_steps)
            def _():
                issue(step + 1, 1 - slot)

            wait(slot)
            rt = rt_ref[0]
            moe = (rt[:, 0:1] * _load_tile_rows(buf.at[slot, 0], tc)
                   + rt[:, 1:2] * _load_tile_rows(buf.at[slot, 1], tc))
            o_ref[0] = x1_ref[0] + g2_ref[0] * moe


def _combine(dest0, dest1, yb, x1, route, g2):
    bn, s, d = x1.shape
    tc = min(COMBINE_ROWS, s)
    row = lambda b, i, d0, d1: (b, i, 0)
    grid_spec = pltpu.PrefetchScalarGridSpec(
        num_scalar_prefetch=2,
        grid=(bn, s // tc),
        in_specs=[pl.BlockSpec(memory_space=pl.ANY),
                  pl.BlockSpec((1, tc, d), row),
                  pl.BlockSpec((1, tc, ROUTE_LANES), row),
                  pl.BlockSpec((1, 1, d), lambda b, i, d0, d1: (b, 0, 0))],
        out_specs=pl.BlockSpec((1, tc, d), row),
        scratch_shapes=[pltpu.VMEM((2, 2, tc * ROW_TILES, 128), f32),
                        pltpu.SemaphoreType.DMA((2,))],
    )
    return pl.pallas_call(
        _combine_kernel,
        grid_spec=grid_spec,
        out_shape=jax.ShapeDtypeStruct((bn, s, d), f32),
        compiler_params=_cparams(("arbitrary", "arbitrary")),
        name="combine",
    )(dest0, dest1, yb, x1, route, g2.reshape(bn, 1, d))


def _dispatch_plan(route, counts):
    n_tok = route.shape[0]
    counts = counts.astype(jnp.int32)
    padded = (counts + MOE_ROWS - 1) // MOE_ROWS * MOE_ROWS
    pad_ends = jnp.cumsum(padded)
    pad_starts = pad_ends - padded
    starts = jnp.cumsum(counts) - counts
    n_blocks = (2 * n_tok + N_EXPERTS * MOE_ROWS) // MOE_ROWS
    eid = route[:, 2:4].astype(jnp.int32)
    rank = route[:, 4:6].astype(jnp.int32)
    hot = eid[:, :, None] == jnp.arange(N_EXPERTS, dtype=jnp.int32)
    dest = rank + jnp.sum(jnp.where(hot, pad_starts, 0), axis=-1)
    tok = jnp.broadcast_to(jnp.arange(n_tok, dtype=jnp.int32)[:, None], (n_tok, 2))
    _, tok_sorted = lax.sort((dest.reshape(-1), tok.reshape(-1)), num_keys=1)
    tok_sorted = jnp.concatenate([tok_sorted, jnp.zeros((MOE_ROWS,), jnp.int32)])
    block_row = jnp.arange(n_blocks, dtype=jnp.int32) * MOE_ROWS
    block_e = jnp.minimum(jnp.sum(pad_ends[None, :] <= block_row[:, None], axis=-1),
                          N_EXPERTS - 1).astype(jnp.int32)
    hot_b = block_e[:, None] == jnp.arange(N_EXPERTS, dtype=jnp.int32)
    block_j0 = block_row + jnp.sum(jnp.where(hot_b, starts - pad_starts, 0), axis=-1)
    block_j0 = jnp.clip(block_j0, 0, 2 * n_tok - 1).astype(jnp.int32)
    return tok_sorted, block_e, block_j0, dest[:, 0], dest[:, 1]


def kernel(x, c, w_ada, b_ada, norm1_g, w_in, rwkv_mu, rwkv_w0, rwkv_w_up, rwkv_a0, rwkv_a_up, rwkv_g_up, rwkv_k_k, rwkv_k_a, rwkv_r_k, rwkv_lnx_g, rwkv_lnx_b, attn_q_g, attn_k_g, attn_rel_bias, w_branch_rwkv, w_branch_attn, w_out, norm2_g, router_coarse_w, router_coarse_b, router_fine_w, router_fine_b, expert_w_gate, expert_w_up, expert_w_down):
    bn, s, d = x.shape
    depth = w_ada.shape[0]
    for l in range(depth):
        mod = _ada(c, w_ada[l], b_ada[l])
        sh1, sc1, g1, sh2, sc2, g2 = [mod[:, i * d:(i + 1) * d] for i in range(6)]

        p_rwkv, q, k, v, gates = _inproj(x, norm1_g[l], sh1, sc1, w_in[l].astype(bf16),
                                         attn_q_g[l], attn_k_g[l])
        y_r = _rwkv(p_rwkv, rwkv_mu[l], rwkv_w0[l], rwkv_w_up[l], rwkv_a0[l], rwkv_a_up[l],
                    rwkv_g_up[l], rwkv_k_k[l], rwkv_k_a[l], rwkv_r_k[l].reshape(-1),
                    rwkv_lnx_g[l], rwkv_lnx_b[l])
        y_a = _attn(q, k, v, attn_rel_bias[l])

        n_rt = N_GROUPS + N_EXPERTS
        w_rt = jnp.zeros((d, ROUTE_LANES), f32)
        w_rt = w_rt.at[:, :N_GROUPS].set(router_coarse_w[l]).at[:, N_GROUPS:n_rt].set(router_fine_w[l])
        w_rt_hi = w_rt.astype(bf16)
        w_rt = jnp.concatenate([w_rt_hi, (w_rt - w_rt_hi.astype(f32)).astype(bf16)], axis=1)
        b_rt = jnp.zeros((1, ROUTE_LANES), f32)
        b_rt = b_rt.at[0, :N_GROUPS].set(router_coarse_b[l]).at[0, N_GROUPS:n_rt].set(router_fine_b[l])
        x1, h2, route, counts = _merge(y_r, y_a, gates, x, g1, sh2, sc2, norm2_g[l],
                                       w_branch_rwkv[l].astype(bf16),
                                       w_branch_attn[l].astype(bf16),
                                       w_out[l].astype(bf16), w_rt, b_rt)

        tok_sorted, block_e, block_j0, dest0, dest1 = _dispatch_plan(
            route.reshape(bn * s, ROUTE_LANES), counts[0, N_GROUPS:n_rt])
        yb = _moe(h2.reshape(bn * s, ROW_TILES, 128), tok_sorted, block_e, block_j0,
                  expert_w_gate[l], expert_w_up[l], expert_w_down[l])
        x = _combine(dest0, dest1, yb, x1, route, g2)
    return x
```

```python
import functools

import numpy as np
import jax
import jax.numpy as jnp
from jax import lax
from jax.experimental import pallas as pl
from jax.experimental.pallas import tpu as pltpu

f32 = jnp.float32
bf16 = jnp.bfloat16
HI = lax.Precision.HIGHEST

D_MODEL = 1024
HEAD_DIM = 64
N_HEADS = 8
WIDTH = N_HEADS * HEAD_DIM
DECAY_LORA = 64
ICLR_LORA = 64
GATE_LORA = 128
RWKV_COLS = 3 * WIDTH + DECAY_LORA + ICLR_LORA + GATE_LORA
ATT_COLS = 3 * WIDTH
GATE_COLS = 2 * D_MODEL
CHUNK = 64
PREV_CHUNKS = 8
REL_CLIP = 128
N_GROUPS = 4
EXPERTS_PER_GROUP = 8
N_EXPERTS = N_GROUPS * EXPERTS_PER_GROUP
D_EXPERT = 512
NORM_EPS = 1e-6
GN_EPS = 64e-5
NEG_INF = -1e30

INPROJ_ROWS = 256
RWKV_CHUNK = 64
RWKV_BATCH_ROWS = 4
ATT_QROWS = 128
ATT_WINDOW = PREV_CHUNKS * CHUNK + ATT_QROWS
MERGE_ROWS = 512
MERGE_PART = 512
MOE_ROWS = 512
COMBINE_ROWS = 128
ROUTE_LANES = 128
ROW_TILES = D_MODEL // 128
VMEM_LIMIT = 56 * 1024 * 1024


def _cparams(sem):
    return pltpu.CompilerParams(dimension_semantics=sem, vmem_limit_bytes=VMEM_LIMIT)


def _sigmoid(x):
    return 1.0 / (1.0 + jnp.exp(-x))


def _ada_kernel(c_ref, w_ref, b_ref, o_ref):
    c = c_ref[...]
    s = c * _sigmoid(c)
    o_ref[...] = jnp.dot(s, w_ref[...], precision=HI, preferred_element_type=f32) + b_ref[...]


def _ada(c, w, b):
    bn, d = c.shape
    n = w.shape[1]
    tn = 1024
    return pl.pallas_call(
        _ada_kernel,
        grid=(n // tn,),
        in_specs=[pl.BlockSpec((bn, d), lambda j: (0, 0)),
                  pl.BlockSpec((d, tn), lambda j: (0, j)),
                  pl.BlockSpec((1, tn), lambda j: (0, j))],
        out_specs=pl.BlockSpec((bn, tn), lambda j: (0, j)),
        out_shape=jax.ShapeDtypeStruct((bn, n), f32),
        compiler_params=_cparams(("arbitrary",)),
        name="ada",
    )(c, w, b.reshape(1, n))


def _rms_mod(x, gain, shift, scale):
    ms = jnp.mean(x * x, axis=-1, keepdims=True)
    h = x * lax.rsqrt(ms + NORM_EPS) * gain
    return h * (1.0 + scale) + shift


def _head_norm(x, gain, ones_bd):
    outs = []
    for c in range(WIDTH // 128):
        xc = x[:, c * 128:(c + 1) * 128]
        ms = jnp.dot((xc * xc).astype(bf16), ones_bd, preferred_element_type=f32) * (1.0 / HEAD_DIM)
        outs.append(xc * lax.rsqrt(ms + NORM_EPS))
    return jnp.concatenate(outs, axis=1) * gain


def _inproj_kernel(x_ref, g_ref, sh_ref, sc_ref, w_ref, qg_ref, kg_ref, bd_ref,
                   pr_ref, q_ref, k_ref, v_ref, pg_ref):
    h = _rms_mod(x_ref[0], g_ref[...], sh_ref[0], sc_ref[0]).astype(bf16)
    c1, c2 = RWKV_COLS, RWKV_COLS + ATT_COLS
    pr_ref[0] = jnp.dot(h, w_ref[:, :c1], preferred_element_type=f32)
    pa = jnp.dot(h, w_ref[:, c1:c2], preferred_element_type=f32)
    ones_bd = bd_ref[...]
    q_ref[0] = _head_norm(pa[:, :WIDTH], qg_ref[...], ones_bd).astype(bf16)
    k_ref[0] = _head_norm(pa[:, WIDTH:2 * WIDTH], kg_ref[...], ones_bd).astype(bf16)
    v_ref[0] = pa[:, 2 * WIDTH:].astype(bf16)
    pg_ref[0] = _sigmoid(jnp.dot(h, w_ref[:, c2:], preferred_element_type=f32))


def _inproj(x, gain, shift, scale, w_bf, q_g, k_g):
    bn, s, d = x.shape
    tm = min(INPROJ_ROWS, s)
    n = w_bf.shape[1]
    row = lambda b, i: (b, i, 0)
    per_b = lambda b, i: (b, 0, 0)
    const = lambda b, i: (0, 0)
    per_head = lambda g, scale: (jnp.tile(g.astype(f32), N_HEADS) * scale).reshape(1, WIDTH)
    lane_head = np.arange(128) // HEAD_DIM
    ones_bd = jnp.asarray(lane_head[:, None] == lane_head[None, :], bf16)
    return pl.pallas_call(
        _inproj_kernel,
        grid=(bn, s // tm),
        in_specs=[pl.BlockSpec((1, tm, d), row),
                  pl.BlockSpec((1, d), const),
                  pl.BlockSpec((1, 1, d), per_b),
                  pl.BlockSpec((1, 1, d), per_b),
                  pl.BlockSpec((d, n), const),
                  pl.BlockSpec((1, WIDTH), const),
                  pl.BlockSpec((1, WIDTH), const),
                  pl.BlockSpec((128, 128), const)],
        out_specs=[pl.BlockSpec((1, tm, RWKV_COLS), row),
                   pl.BlockSpec((1, tm, WIDTH), row),
                   pl.BlockSpec((1, tm, WIDTH), row),
                   pl.BlockSpec((1, tm, WIDTH), row),
                   pl.BlockSpec((1, tm, GATE_COLS), row)],
        out_shape=[jax.ShapeDtypeStruct((bn, s, RWKV_COLS), f32),
                   jax.ShapeDtypeStruct((bn, s, WIDTH), bf16),
                   jax.ShapeDtypeStruct((bn, s, WIDTH), bf16),
                   jax.ShapeDtypeStruct((bn, s, WIDTH), bf16),
                   jax.ShapeDtypeStruct((bn, s, GATE_COLS), f32)],
        compiler_params=_cparams(("parallel", "parallel")),
        name="inproj",
    )(x, gain.reshape(1, d), shift.reshape(bn, 1, d), scale.reshape(bn, 1, d), w_bf,
      per_head(q_g, HEAD_DIM ** -0.5), per_head(k_g, 1.0), ones_bd)


def _split_bf16(x):
    hi = x.astype(bf16)
    return hi, (x - hi.astype(f32)).astype(bf16)


def _mm_split(x, w2):
    rows, n = x.shape[0], w2.shape[1] // 2
    pr = jnp.dot(jnp.concatenate(_split_bf16(x), axis=0), w2, preferred_element_type=f32)
    return (pr[:rows, :n] + pr[rows:, :n]) + (pr[:rows, n:] + pr[rows:, n:])


def _bmm(a, b):
    return jnp.dot(a, b, preferred_element_type=f32)


def _bmm_nt(a, b):
    return lax.dot_general(a, b, (((1,), (1,)), ((), ())), preferred_element_type=f32)


def _bmm_tn(a, b):
    return lax.dot_general(a, b, (((0,), (0,)), ((), ())), preferred_element_type=f32)


def _rwkv_kernel(p_ref, mu_ref, w0_ref, wup_ref, a0_ref, aup_ref, gup_ref, kk_ref, ka_ref,
                 rk_ref, lg_ref, lb_ref, o_ref, last_ref, st_ref):
    C = RWKV_CHUNK
    c_idx = pl.program_id(1)

    @pl.when(c_idx == 0)
    def _():
        last_ref[...] = jnp.zeros_like(last_ref)
        st_ref[...] = jnp.zeros_like(st_ref)

    NB = p_ref.shape[0]
    R = NB * C
    rows_of = [slice(b * C, (b + 1) * C) for b in range(NB)]
    pc = p_ref[...].reshape(R, RWKV_COLS)
    row_full = lax.broadcasted_iota(jnp.int32, pc.shape, 0)
    prev = pltpu.roll(pc, 1, axis=0)
    for b in range(NB):
        prev = jnp.where(row_full == b * C, last_ref[b:b + 1, :], prev)
        last_ref[b:b + 1, :] = pc[(b + 1) * C - 1:(b + 1) * C, :]
    ps = pc + (prev - pc) * mu_ref[...]

    o1, o2, o3 = WIDTH, 2 * WIDTH, 3 * WIDTH
    o4, o5 = o3 + DECAY_LORA, o3 + DECAY_LORA + ICLR_LORA
    r, k, v = ps[:, :o1], ps[:, o1:o2], ps[:, o2:o3]
    wd, ad, gd = ps[:, o3:o4], ps[:, o4:o5], ps[:, o5:]

    wraw = w0_ref[...] + _mm_split(jnp.tanh(wd), wup_ref[...])
    z = -wraw
    softplus = jnp.maximum(z, 0.0) + jnp.log(1.0 + jnp.exp(-jnp.abs(z)))
    logdec = -jnp.exp(-softplus - 0.5)
    a = _sigmoid(a0_ref[...] + _mm_split(ad, aup_ref[...]))
    g = _mm_split(_sigmoid(gd), gup_ref[...])
    kkr = k * kk_ref[...]
    k2 = k * (1.0 + (a - 1.0) * ka_ref[...])
    rkk = r * k2 * rk_ref[...]

    ri = lax.broadcasted_iota(jnp.int32, (R, R), 0)
    ci = lax.broadcasted_iota(jnp.int32, (R, R), 1)
    tri = ((ri >= ci) & (ri // C == ci // C)).astype(bf16)
    ld_hi = logdec.astype(bf16)
    ld_r = logdec - ld_hi.astype(f32)
    ld_mid = ld_r.astype(bf16)
    ld_lo = (ld_r - ld_mid.astype(f32)).astype(bf16)
    cum3 = _bmm(tri, jnp.concatenate([ld_hi, ld_mid, ld_lo], axis=1))
    cum = cum3[:, :WIDTH] + cum3[:, WIDTH:2 * WIDTH] + cum3[:, 2 * WIDTH:]
    e_pos = jnp.exp(cum)
    e_prev = jnp.exp(cum - logdec)
    e_neg = jnp.exp(-cum)
    cum_ends = [cum[(b + 1) * C - 1:(b + 1) * C, :] for b in range(NB)]
    e_end = jnp.exp(jnp.concatenate([jnp.broadcast_to(ce, (C, WIDTH)) for ce in cum_ends], axis=0)
                    - cum)
    g_end = [jnp.exp(ce) for ce in cum_ends]

    C2 = 2 * C
    lane1 = lax.broadcasted_iota(jnp.int32, (1, 128), 1)
    keep_lo = (lane1 < HEAD_DIM).astype(bf16)
    keep_hi = (lane1 >= HEAD_DIM).astype(bf16)

    def bd(x):
        return jnp.concatenate([x * keep_lo, x * keep_hi], axis=0)

    r2 = lax.broadcasted_iota(jnp.int32, (C2, C2), 0)
    c2 = lax.broadcasted_iota(jnp.int32, (C2, C2), 1)
    ones_bd = ((r2 // HEAD_DIM) == (c2 // HEAD_DIM)).astype(bf16)
    eye = (r2 == c2).astype(f32)
    rr, cc = r2 % C, c2 % C
    strict = rr > cc
    blk16 = (rr // 16) == (cc // 16)
    blk32 = (rr // 32) == (cc // 32)
    m_diag16 = strict & blk16
    m_off32 = strict & blk32 & jnp.logical_not(blk16)
    m_off64 = strict & jnp.logical_not(blk32)
    r4 = lax.broadcasted_iota(jnp.int32, (C2, 2 * C2), 0) % C
    c4 = lax.broadcasted_iota(jnp.int32, (C2, 2 * C2), 1) % C
    strict2 = r4 > c4
    incl2 = r4 >= c4
    zeros_bd = jnp.zeros((C2, C2), bf16)

    P = range(N_HEADS // 2)
    ps_ = [slice(p * 128, (p + 1) * 128) for p in P]

    def head_sums(x):
        return jnp.concatenate([_bmm(x[:, s], ones_bd) for s in ps_], axis=1)

    kkn = kkr * lax.rsqrt(jnp.maximum(head_sums((kkr * kkr).astype(bf16)), 1e-24))
    bh = kkn * a
    at_w = (-kkn * e_prev).astype(bf16)
    rt_w = (r * e_pos).astype(bf16)
    bt_w = (bh * e_neg).astype(bf16)
    kt_w = (k2 * e_neg).astype(bf16)
    bhat_w = (bh * e_end).astype(bf16)
    khat_w = (k2 * e_end).astype(bf16)
    v_w = v.astype(bf16)

    units = [(b, p) for b in range(NB) for p in P]
    P = range(len(units))

    def sub(x, q):
        return x[rows_of[units[q][0]], ps_[units[q][1]]]

    at = [bd(sub(at_w, q)) for q in P]
    rt = [bd(sub(rt_w, q)) for q in P]
    vb = [bd(sub(v_w, q)) for q in P]
    smat = [_bmm_nt(jnp.concatenate([at[p], rt[p]], axis=0),
                    jnp.concatenate([bd(sub(bt_w, p)), bd(sub(kt_w, p))], axis=0)) for p in P]
    n_aak = [jnp.where(strict2, smat[p][:C2], 0.0) for p in P]
    arb_ark = [jnp.where(incl2, smat[p][C2:], 0.0).astype(bf16) for p in P]
    nmat = [n_aak[p][:, :C2] for p in P]

    pw = [jnp.where(m_diag16, nmat[p], 0.0) for p in P]
    tinv = [eye + pw[p] for p in P]
    for _ in range(3):
        pwb = [pw[p].astype(bf16) for p in P]
        pw = [_bmm(pwb[p], pwb[p]) for p in P]
        tinv = [tinv[p] + _bmm(tinv[p].astype(bf16), pw[p].astype(bf16)) for p in P]
    for m_off in (m_off32, m_off64):
        tb = [tinv[p].astype(bf16) for p in P]
        mid = [_bmm(jnp.where(m_off, nmat[p], 0.0).astype(bf16), tb[p]).astype(bf16) for p in P]
        tinv = [tinv[p] + _bmm(tb[p], mid[p]) for p in P]
    tb = [tinv[p].astype(bf16) for p in P]

    aakv = [_bmm(n_aak[p][:, C2:].astype(bf16), vb[p]).astype(bf16) for p in P]
    m1u0 = [_bmm(tb[p], jnp.concatenate([at[p], aakv[p]], axis=1)).astype(bf16) for p in P]
    rhs = [jnp.concatenate([m1u0[p], jnp.concatenate([zeros_bd, vb[p]], axis=1)], axis=0)
           for p in P]
    my = [_bmm(arb_ark[p], rhs[p]) for p in P]
    gd_ = [_bmm_tn(jnp.concatenate([bd(sub(bhat_w, p)), bd(sub(khat_w, p))], axis=0), rhs[p])
           for p in P]
    m2 = [(rt[p].astype(f32) + my[p][:, :C2]).astype(bf16) for p in P]
    gmat = [(eye * g_end[units[p][0]][:, ps_[units[p][1]]] + gd_[p][:, :C2]).astype(bf16)
            for p in P]
    upd = [_bmm(jnp.concatenate([m2[p], gmat[p]], axis=0), st_ref[p].astype(bf16)) for p in P]

    lane_c = lax.broadcasted_iota(jnp.int32, (C, 128), 1)
    ys = []
    for p in P:
        st_ref[p] = upd[p][C2:] + gd_[p][:, C2:]
        y_bd = upd[p][:C2] + my[p][:, C2:]
        ys.append(jnp.where(lane_c < HEAD_DIM, y_bd[:C], y_bd[C:]))
    n_pairs = len(ps_)
    y = jnp.concatenate([jnp.concatenate(ys[b * n_pairs:(b + 1) * n_pairs], axis=1)
                         for b in range(NB)], axis=0)

    y_hi = y.astype(bf16)
    y_lo = (y - y_hi.astype(f32)).astype(bf16)
    sums = head_sums(jnp.concatenate([y_hi, y_lo, rkk.astype(bf16)], axis=0))
    yc = y - (sums[:R] + sums[R:2 * R]) * (1.0 / HEAD_DIM)
    var = head_sums((yc * yc).astype(bf16)) * (1.0 / HEAD_DIM)
    yn = yc * lax.rsqrt(var + GN_EPS) * lg_ref[...] + lb_ref[...]
    o_ref[...] = ((yn + sums[2 * R:] * v) * g).reshape(NB, C, WIDTH)


def _rwkv(p_rwkv, mu, w0, w_up, a0, a_up, g_up, k_k, k_a, r_k, lnx_g, lnx_b):
    bn, s, _ = p_rwkv.shape
    C = RWKV_CHUNK
    vec = lambda t: t.reshape(1, -1)
    const = lambda b, c: (0, 0)
    full = lambda t: pl.BlockSpec(t.shape, const)
    hi_lo = lambda w: jnp.concatenate(_split_bf16(w.astype(f32)), axis=1)
    args = [vec(mu), vec(w0), hi_lo(w_up), vec(a0), hi_lo(a_up), hi_lo(g_up), vec(k_k), vec(k_a),
            vec(r_k), vec(lnx_g), vec(lnx_b)]
    nb = RWKV_BATCH_ROWS if bn % RWKV_BATCH_ROWS == 0 else 1
    return pl.pallas_call(
        _rwkv_kernel,
        grid=(bn // nb, s // C),
        in_specs=[pl.BlockSpec((nb, C, RWKV_COLS), lambda b, c: (b, c, 0))] + [full(t) for t in args],
        out_specs=pl.BlockSpec((nb, C, WIDTH), lambda b, c: (b, c, 0)),
        out_shape=jax.ShapeDtypeStruct((bn, s, WIDTH), f32),
        scratch_shapes=[pltpu.VMEM((nb, RWKV_COLS), f32),
                        pltpu.VMEM((nb * N_HEADS // 2, 2 * HEAD_DIM, 2 * HEAD_DIM), f32)],
        compiler_params=_cparams(("parallel", "arbitrary")),
        name="rwkv",
    )(p_rwkv, *args)


def _attn_kernel(q_ref, k_ref, v_ref, bias_ref, o_ref):
    TQ = ATT_QROWS
    t = pl.program_id(1)
    g0 = pl.multiple_of(jnp.maximum(t - PREV_CHUNKS * CHUNK // TQ, 0) * TQ, TQ)
    q = q_ref[0]
    lane = lax.broadcasted_iota(jnp.int32, (1, 128), 1)
    keep_lo = (lane < HEAD_DIM).astype(bf16)
    keep_hi = (lane >= HEAD_DIM).astype(bf16)
    lane_f = lax.broadcasted_iota(jnp.int32, (TQ, 128), 1)
    pairs = range(N_HEADS // 2)

    scores = []
    for p in pairs:
        qp = q[:, p * 128:(p + 1) * 128]
        qs = jnp.concatenate([qp * keep_lo, qp * keep_hi], axis=0)
        kp = k_ref[0, pl.ds(g0, ATT_WINDOW), p * 128:(p + 1) * 128]
        s = lax.dot_general(qs, kp, (((1,), (1,)), ((), ())), preferred_element_type=f32)
        scores.append(s + jnp.concatenate([bias_ref[0, 2 * p], bias_ref[0, 2 * p + 1]], axis=0))
    probs, denoms = [], []
    for p in pairs:
        m = jnp.max(scores[p], axis=-1, keepdims=True)
        e = jnp.exp(scores[p] - m)
        denoms.append(jnp.sum(e, axis=-1, keepdims=True))
        probs.append(e.astype(bf16))
    outs = []
    for p in pairs:
        vp = v_ref[0, pl.ds(g0, ATT_WINDOW), p * 128:(p + 1) * 128]
        o2 = jnp.dot(probs[p], vp, preferred_element_type=f32) / denoms[p]
        outs.append(jnp.where(lane_f < HEAD_DIM, o2[:TQ], o2[TQ:]))
    o_ref[0] = jnp.concatenate(outs, axis=-1)


def _attn_bias_tables(rel_bias):
    TQ, W = ATT_QROWS, ATT_WINDOW
    n_var = PREV_CHUNKS * CHUNK // TQ + 1
    n_h = rel_bias.shape[0]
    rb = rel_bias.astype(f32)
    ext = jnp.concatenate([jnp.broadcast_to(rb[:, :1], (n_h, W - REL_CLIP)), rb,
                           jnp.broadcast_to(rb[:, -1:], (n_h, W - REL_CLIP - 1))], axis=1)
    width = W + (n_var - 1) * TQ
    toep = jnp.stack([ext[:, TQ - qi: TQ - qi + width] for qi in range(TQ)], axis=1)
    qc = np.arange(TQ)[:, None] // CHUNK
    kc = np.arange(W)[None, :] // CHUNK
    tables = []
    for v in range(n_var):
        off = (n_var - 1 - v) * TQ
        rel_chunk = kc - qc - v * (TQ // CHUNK)
        visible = (rel_chunk <= 0) & (rel_chunk >= -PREV_CHUNKS)
        tables.append(jnp.where(jnp.asarray(visible)[None], toep[:, :, off: off + W], NEG_INF))
    return jnp.stack(tables, axis=0)


def _attn(q, k, v, rel_bias):
    bn, s, _ = q.shape
    TQ = ATT_QROWS
    bias = _attn_bias_tables(rel_bias)
    n_var = bias.shape[0]
    tile = pl.BlockSpec((1, TQ, WIDTH), lambda b, t: (b, t, 0))
    whole = pl.BlockSpec((1, s, WIDTH), lambda b, t: (b, 0, 0))
    return pl.pallas_call(
        _attn_kernel,
        grid=(bn, s // TQ),
        in_specs=[tile, whole, whole,
                  pl.BlockSpec((1, N_HEADS, TQ, ATT_WINDOW),
                               lambda b, t: (jnp.minimum(t, n_var - 1), 0, 0, 0))],
        out_specs=tile,
        out_shape=jax.ShapeDtypeStruct((bn, s, WIDTH), f32),
        compiler_params=_cparams(("parallel", "arbitrary")),
        name="attn",
    )(q, k, v, bias)


def _store_tile_rows(ref, x):
    rows = x.shape[0]
    for j in range(ROW_TILES):
        ref[pl.ds(j, rows, stride=ROW_TILES), :] = x[:, j * 128:(j + 1) * 128]


def _load_tile_rows(ref, rows):
    return jnp.concatenate(
        [ref[pl.ds(j, rows, stride=ROW_TILES), :] for j in range(ROW_TILES)], axis=1)


def _route(logits, tri, counts):
    lane = lax.broadcasted_iota(jnp.int32, logits.shape, 1).astype(f32)
    big = float(ROUTE_LANES)
    is_c = lane < N_GROUPS
    cl = jnp.where(is_c, logits, NEG_INF)
    cm = jnp.max(cl, axis=-1, keepdims=True)
    grp = jnp.min(jnp.where(cl == cm, lane, big), axis=-1, keepdims=True)
    p_grp = 1.0 / jnp.sum(jnp.where(is_c, jnp.exp(cl - cm), 0.0), axis=-1, keepdims=True)
    lo = N_GROUPS + grp * EXPERTS_PER_GROUP
    in_g = (lane >= lo) & (lane < lo + EXPERTS_PER_GROUP)
    fl = jnp.where(in_g, logits, NEG_INF)
    f1 = jnp.max(fl, axis=-1, keepdims=True)
    i1 = jnp.min(jnp.where(in_g & (fl == f1), lane, big), axis=-1, keepdims=True)
    fl2 = jnp.where(lane == i1, NEG_INF, fl)
    in_g2 = in_g & (lane != i1)
    f2 = jnp.max(fl2, axis=-1, keepdims=True)
    i2 = jnp.min(jnp.where(in_g2 & (fl2 == f2), lane, big), axis=-1, keepdims=True)
    e2 = jnp.exp(f2 - f1)
    w1 = p_grp / (1.0 + e2)
    w2 = p_grp * e2 / (1.0 + e2)

    hot1 = (lane == i1).astype(f32)
    hot2 = (lane == i2).astype(f32)
    pre1 = jnp.dot(tri, hot1.astype(bf16), preferred_element_type=f32)
    pre2 = jnp.dot(tri, hot2.astype(bf16), preferred_element_type=f32)
    tot1 = jnp.sum(hot1, axis=0, keepdims=True)
    tot2 = jnp.sum(hot2, axis=0, keepdims=True)
    rank1 = jnp.sum(hot1 * (pre1 + counts), axis=-1, keepdims=True)
    rank2 = jnp.sum(hot2 * (pre2 + counts + tot1), axis=-1, keepdims=True)

    out = jnp.where(lane == 0, w1, 0.0)
    out = jnp.where(lane == 1, w2, out)
    out = jnp.where(lane == 2, i1 - N_GROUPS, out)
    out = jnp.where(lane == 3, i2 - N_GROUPS, out)
    out = jnp.where(lane == 4, rank1, out)
    out = jnp.where(lane == 5, rank2, out)
    return out, counts + tot1 + tot2


def _merge_kernel(yr_ref, ya_ref, gt_ref, x_ref, g1_ref, sh_ref, sc_ref, ng_ref, wbr_ref,
                  wba_ref, wout_ref, wrt_ref, brt_ref, tri_ref, x1_ref, h2_ref, route_ref,
                  cnt_ref):
    @pl.when((pl.program_id(0) == 0) & (pl.program_id(1) == 0))
    def _():
        cnt_ref[...] = jnp.zeros_like(cnt_ref)

    part = tri_ref.shape[0]
    counts = cnt_ref[...]
    for i in range(x_ref.shape[1] // part):
        rs = pl.ds(i * part, part)
        t1 = jnp.dot(yr_ref[0, rs, :].astype(bf16), wbr_ref[...], preferred_element_type=f32)
        t2 = jnp.dot(ya_ref[0, rs, :].astype(bf16), wba_ref[...], preferred_element_type=f32)
        gt = gt_ref[0, rs, :]
        m = gt[:, :D_MODEL] * t1 + gt[:, D_MODEL:] * t2
        mixed = jnp.dot(m.astype(bf16), wout_ref[...], preferred_element_type=f32)
        x1 = x_ref[0, rs, :] + g1_ref[0] * mixed
        x1_ref[0, rs, :] = x1
        h2 = _rms_mod(x1, ng_ref[...], sh_ref[0], sc_ref[0])
        _store_tile_rows(h2_ref.at[0, pl.ds(i * part * ROW_TILES, part * ROW_TILES)], h2)
        h2_hi = h2.astype(bf16)
        h2_lo = (h2 - h2_hi.astype(f32)).astype(bf16)
        pa = jnp.dot(h2_hi, wrt_ref[...], preferred_element_type=f32)
        pb = jnp.dot(h2_lo, wrt_ref[...], preferred_element_type=f32)
        logits = ((pa[:, :ROUTE_LANES] + pb[:, :ROUTE_LANES])
                  + (pa[:, ROUTE_LANES:] + pb[:, ROUTE_LANES:]) + brt_ref[...])
        route, counts = _route(logits, tri_ref[...], counts)
        route_ref[0, rs, :] = route
    cnt_ref[...] = counts


def _merge(y_r, y_a, gates, x, g1, sh2, sc2, norm2_g, w_br, w_ba, w_out, w_rt, b_rt):
    bn, s, d = x.shape
    tm = min(MERGE_ROWS, s)
    row = lambda b, i: (b, i, 0)
    per_b = lambda b, i: (b, 0, 0)
    const = lambda b, i: (0, 0)
    full = lambda t: pl.BlockSpec(t.shape, const)
    part = min(MERGE_PART, tm)
    tri = jnp.asarray(np.tril(np.ones((part, part), np.float32), -1), bf16)
    return pl.pallas_call(
        _merge_kernel,
        grid=(bn, s // tm),
        in_specs=[pl.BlockSpec((1, tm, WIDTH), row),
                  pl.BlockSpec((1, tm, WIDTH), row),
                  pl.BlockSpec((1, tm, GATE_COLS), row),
                  pl.BlockSpec((1, tm, d), row),
                  pl.BlockSpec((1, 1, d), per_b),
                  pl.BlockSpec((1, 1, d), per_b),
                  pl.BlockSpec((1, 1, d), per_b),
                  pl.BlockSpec((1, d), const),
                  full(w_br), full(w_ba), full(w_out), full(w_rt), full(b_rt), full(tri)],
        out_specs=[pl.BlockSpec((1, tm, d), row),
                   pl.BlockSpec((1, tm * ROW_TILES, 128), row),
                   pl.BlockSpec((1, tm, ROUTE_LANES), row),
                   pl.BlockSpec((1, ROUTE_LANES), const)],
        out_shape=[jax.ShapeDtypeStruct((bn, s, d), f32),
                   jax.ShapeDtypeStruct((bn, s * ROW_TILES, 128), f32),
                   jax.ShapeDtypeStruct((bn, s, ROUTE_LANES), f32),
                   jax.ShapeDtypeStruct((1, ROUTE_LANES), f32)],
        compiler_params=_cparams(("arbitrary", "arbitrary")),
        name="merge",
    )(y_r, y_a, gates, x, g1.reshape(bn, 1, d), sh2.reshape(bn, 1, d), sc2.reshape(bn, 1, d),
      norm2_g.reshape(1, d), w_br, w_ba, w_out, w_rt, b_rt, tri)


def _row_copy(src, src_row, dst, dst_row, sem):
    return pltpu.make_async_copy(src.at[src_row], dst.at[pl.ds(dst_row * ROW_TILES, ROW_TILES)],
                                 sem)


def _moe_kernel(tok_ref, be_ref, j0_ref, h2_hbm, wg_ref, wu_ref, wd_ref, o_ref, xbuf, sem):
    del be_ref
    i = pl.program_id(0)
    last = pl.num_programs(0) - 1

    def issue(blk, slot):
        j0 = j0_ref[blk]
        for r in range(MOE_ROWS):
            _row_copy(h2_hbm, tok_ref[j0 + r], xbuf.at[slot], r, sem.at[slot]).start(priority=r % 2)

    def wait(slot):
        for r in range(MOE_ROWS):
            _row_copy(h2_hbm, 0, xbuf.at[slot], r, sem.at[slot]).wait()

    for slot in range(2):
        @pl.when(i % 2 == slot)
        def _(slot=slot):
            if slot == 0:
                @pl.when(i == 0)
                def _():
                    issue(0, 0)

            wait(slot)
            issue(jnp.minimum(i + 1, last), 1 - slot)
            xb = _load_tile_rows(xbuf.at[slot], MOE_ROWS).astype(bf16)
            gate = jnp.dot(xb, wg_ref[0].astype(bf16), preferred_element_type=f32)
            up = jnp.dot(xb, wu_ref[0].astype(bf16), preferred_element_type=f32)
            hid = gate * _sigmoid(gate) * up
            _store_tile_rows(
                o_ref, jnp.dot(hid.astype(bf16), wd_ref[0].astype(bf16),
                               preferred_element_type=f32))

            @pl.when(i == last)
            def _():
                wait(1 - slot)


def _moe(h2_tiles, tok_sorted, block_e, block_j0, wg, wu, wd):
    d = D_MODEL
    n_blocks = block_e.shape[0]
    blk_rows = MOE_ROWS * ROW_TILES
    by_expert = lambda i, tok, be, j0: (be[i], 0, 0)
    grid_spec = pltpu.PrefetchScalarGridSpec(
        num_scalar_prefetch=3,
        grid=(n_blocks,),
        in_specs=[pl.BlockSpec(memory_space=pl.ANY),
                  pl.BlockSpec((1, d, D_EXPERT), by_expert),
                  pl.BlockSpec((1, d, D_EXPERT), by_expert),
                  pl.BlockSpec((1, D_EXPERT, d), by_expert)],
        out_specs=pl.BlockSpec((blk_rows, 128), lambda i, tok, be, j0: (i, 0)),
        scratch_shapes=[pltpu.VMEM((2, blk_rows, 128), f32), pltpu.SemaphoreType.DMA((2,))],
    )
    yb = pl.pallas_call(
        _moe_kernel,
        grid_spec=grid_spec,
        out_shape=jax.ShapeDtypeStruct((n_blocks * blk_rows, 128), f32),
        compiler_params=_cparams(("arbitrary",)),
        name="moe",
    )(tok_sorted, block_e, block_j0, h2_tiles, wg, wu, wd)
    return yb.reshape(n_blocks * MOE_ROWS, ROW_TILES, 128)


def _combine_kernel(d0_ref, d1_ref, yb_hbm, x1_ref, rt_ref, g2_ref, o_ref, buf, sem):
    tc = buf.shape[2] // ROW_TILES
    step = pl.program_id(0) * pl.num_programs(1) + pl.program_id(1)
    n_steps = pl.num_programs(0) * pl.num_programs(1)

    def issue(for_step, slot):
        base = for_step * tc
        for r in range(tc):
            _row_copy(yb_hbm, d0_ref[base + r], buf.at[slot, 0], r, sem.at[slot]).start(priority=0)
            _row_copy(yb_hbm, d1_ref[base + r], buf.at[slot, 1], r, sem.at[slot]).start(priority=1)

    def wait(slot):
        for r in range(tc):
            _row_copy(yb_hbm, 0, buf.at[slot, 0], r, sem.at[slot]).wait()
            _row_copy(yb_hbm, 0, buf.at[slot, 1], r, sem.at[slot]).wait()

    @pl.when(step == 0)
    def _():
        issue(step, 0)

    for slot in range(2):
        @pl.when(step % 2 == slot)
        def _(slot=slot):
            @pl.when(step + 1 < n_steps)
            def _():
                issue(step + 1, 1 - slot)

            wait(slot)
            rt = rt_ref[0]
            moe = (rt[:, 0:1] * _load_tile_rows(buf.at[slot, 0], tc)
                   + rt[:, 1:2] * _load_tile_rows(buf.at[slot, 1], tc))
            o_ref[0] = x1_ref[0] + g2_ref[0] * moe


def _combine(dest0, dest1, yb, x1, route, g2):
    bn, s, d = x1.shape
    tc = min(COMBINE_ROWS, s)
    row = lambda b, i, d0, d1: (b, i, 0)
    grid_spec = pltpu.PrefetchScalarGridSpec(
        num_scalar_prefetch=2,
        grid=(bn, s // tc),
        in_specs=[pl.BlockSpec(memory_space=pl.ANY),
                  pl.BlockSpec((1, tc, d), row),
                  pl.BlockSpec((1, tc, ROUTE_LANES), row),
                  pl.BlockSpec((1, 1, d), lambda b, i, d0, d1: (b, 0, 0))],
        out_specs=pl.BlockSpec((1, tc, d), row),
        scratch_shapes=[pltpu.VMEM((2, 2, tc * ROW_TILES, 128), f32),
                        pltpu.SemaphoreType.DMA((2,))],
    )
    return pl.pallas_call(
        _combine_kernel,
        grid_spec=grid_spec,
        out_shape=jax.ShapeDtypeStruct((bn, s, d), f32),
        compiler_params=_cparams(("arbitrary", "arbitrary")),
        name="combine",
    )(dest0, dest1, yb, x1, route, g2.reshape(bn, 1, d))


def _dispatch_plan(route, counts):
    n_tok = route.shape[0]
    counts = counts.astype(jnp.int32)
    padded = (counts + MOE_ROWS - 1) // MOE_ROWS * MOE_ROWS
    pad_ends = jnp.cumsum(padded)
    pad_starts = pad_ends - padded
    starts = jnp.cumsum(counts) - counts
    n_blocks = (2 * n_tok + N_EXPERTS * MOE_ROWS) // MOE_ROWS
    eid = route[:, 2:4].astype(jnp.int32)
    rank = route[:, 4:6].astype(jnp.int32)
    hot = eid[:, :, None] == jnp.arange(N_EXPERTS, dtype=jnp.int32)
    dest = rank + jnp.sum(jnp.where(hot, pad_starts, 0), axis=-1)
    tok = jnp.broadcast_to(jnp.arange(n_tok, dtype=jnp.int32)[:, None], (n_tok, 2))
    _, tok_sorted = lax.sort((dest.reshape(-1), tok.reshape(-1)), num_keys=1)
    tok_sorted = jnp.concatenate([tok_sorted, jnp.zeros((MOE_ROWS,), jnp.int32)])
    block_row = jnp.arange(n_blocks, dtype=jnp.int32) * MOE_ROWS
    block_e = jnp.minimum(jnp.sum(pad_ends[None, :] <= block_row[:, None], axis=-1),
                          N_EXPERTS - 1).astype(jnp.int32)
    hot_b = block_e[:, None] == jnp.arange(N_EXPERTS, dtype=jnp.int32)
    block_j0 = block_row + jnp.sum(jnp.where(hot_b, starts - pad_starts, 0), axis=-1)
    block_j0 = jnp.clip(block_j0, 0, 2 * n_tok - 1).astype(jnp.int32)
    return tok_sorted, block_e, block_j0, dest[:, 0], dest[:, 1]


def kernel(x, c, w_ada, b_ada, norm1_g, w_in, rwkv_mu, rwkv_w0, rwkv_w_up, rwkv_a0, rwkv_a_up, rwkv_g_up, rwkv_k_k, rwkv_k_a, rwkv_r_k, rwkv_lnx_g, rwkv_lnx_b, attn_q_g, attn_k_g, attn_rel_bias, w_branch_rwkv, w_branch_attn, w_out, norm2_g, router_coarse_w, router_coarse_b, router_fine_w, router_fine_b, expert_w_gate, expert_w_up, expert_w_down):
    bn, s, d = x.shape
    depth = w_ada.shape[0]
    for l in range(depth):
        mod = _ada(c, w_ada[l], b_ada[l])
        sh1, sc1, g1, sh2, sc2, g2 = [mod[:, i * d:(i + 1) * d] for i in range(6)]

        p_rwkv, q, k, v, gates = _inproj(x, norm1_g[l], sh1, sc1, w_in[l].astype(bf16),
                                         attn_q_g[l], attn_k_g[l])
        y_r = _rwkv(p_rwkv, rwkv_mu[l], rwkv_w0[l], rwkv_w_up[l], rwkv_a0[l], rwkv_a_up[l],
                    rwkv_g_up[l], rwkv_k_k[l], rwkv_k_a[l], rwkv_r_k[l].reshape(-1),
                    rwkv_lnx_g[l], rwkv_lnx_b[l])
        y_a = _attn(q, k, v, attn_rel_bias[l])

        n_rt = N_GROUPS + N_EXPERTS
        w_rt = jnp.zeros((d, ROUTE_LANES), f32)
        w_rt = w_rt.at[:, :N_GROUPS].set(router_coarse_w[l]).at[:, N_GROUPS:n_rt].set(router_fine_w[l])
        w_rt_hi = w_rt.astype(bf16)
        w_rt = jnp.concatenate([w_rt_hi, (w_rt - w_rt_hi.astype(f32)).astype(bf16)], axis=1)
        b_rt = jnp.zeros((1, ROUTE_LANES), f32)
        b_rt = b_rt.at[0, :N_GROUPS].set(router_coarse_b[l]).at[0, N_GROUPS:n_rt].set(router_fine_b[l])
        x1, h2, route, counts = _merge(y_r, y_a, gates, x, g1, sh2, sc2, norm2_g[l],
                                       w_branch_rwkv[l].astype(bf16),
                                       w_branch_attn[l].astype(bf16),
                                       w_out[l].astype(bf16), w_rt, b_rt)

        tok_sorted, block_e, block_j0, dest0, dest1 = _dispatch_plan(
            route.reshape(bn * s, ROUTE_LANES), counts[0, N_GROUPS:n_rt])
        yb = _moe(h2.reshape(bn * s, ROW_TILES, 128), tok_sorted, block_e, block_j0,
                  expert_w_gate[l], expert_w_up[l], expert_w_down[l])
        x = _combine(dest0, dest1, yb, x1, route, g2)
    return x
```

```python
import functools

import numpy as np
import jax
import jax.numpy as jnp
from jax import lax
from jax.experimental import pallas as pl
from jax.experimental.pallas import tpu as pltpu

f32 = jnp.float32
bf16 = jnp.bfloat16
HI = lax.Precision.HIGHEST

D_MODEL = 1024
HEAD_DIM = 64
N_HEADS = 8
WIDTH = N_HEADS * HEAD_DIM
DECAY_LORA = 64
ICLR_LORA = 64
GATE_LORA = 128
RWKV_COLS = 3 * WIDTH + DECAY_LORA + ICLR_LORA + GATE_LORA
ATT_COLS = 3 * WIDTH
GATE_COLS = 2 * D_MODEL
CHUNK = 64
PREV_CHUNKS = 8
REL_CLIP = 128
N_GROUPS = 4
EXPERTS_PER_GROUP = 8
N_EXPERTS = N_GROUPS * EXPERTS_PER_GROUP
D_EXPERT = 512
NORM_EPS = 1e-6
GN_EPS = 64e-5
NEG_INF = -1e30

INPROJ_ROWS = 256
RWKV_CHUNK = 64
RWKV_BATCH_ROWS = 4
ATT_QROWS = 128
ATT_WINDOW = PREV_CHUNKS * CHUNK + ATT_QROWS
MERGE_ROWS = 512
MERGE_PART = 512
MOE_ROWS = 512
MOE_CHUNK = 64
COMBINE_ROWS = 128
ROUTE_LANES = 128
ROW_TILES = D_MODEL // 128
VMEM_LIMIT = 56 * 1024 * 1024


def _cparams(sem):
    return pltpu.CompilerParams(dimension_semantics=sem, vmem_limit_bytes=VMEM_LIMIT)


def _sigmoid(x):
    return 1.0 / (1.0 + jnp.exp(-x))


def _ada_kernel(c_ref, w_ref, b_ref, o_ref):
    c = c_ref[...]
    s = c * _sigmoid(c)
    o_ref[...] = jnp.dot(s, w_ref[...], precision=HI, preferred_element_type=f32) + b_ref[...]


def _ada(c, w, b):
    bn, d = c.shape
    n = w.shape[1]
    tn = 1024
    return pl.pallas_call(
        _ada_kernel,
        grid=(n // tn,),
        in_specs=[pl.BlockSpec((bn, d), lambda j: (0, 0)),
                  pl.BlockSpec((d, tn), lambda j: (0, j)),
                  pl.BlockSpec((1, tn), lambda j: (0, j))],
        out_specs=pl.BlockSpec((bn, tn), lambda j: (0, j)),
        out_shape=jax.ShapeDtypeStruct((bn, n), f32),
        compiler_params=_cparams(("arbitrary",)),
        name="ada",
    )(c, w, b.reshape(1, n))


def _rms_mod(x, gain, shift, scale):
    ms = jnp.mean(x * x, axis=-1, keepdims=True)
    h = x * lax.rsqrt(ms + NORM_EPS) * gain
    return h * (1.0 + scale) + shift


def _head_norm(x, gain, ones_bd):
    outs = []
    for c in range(WIDTH // 128):
        xc = x[:, c * 128:(c + 1) * 128]
        ms = jnp.dot((xc * xc).astype(bf16), ones_bd, preferred_element_type=f32) * (1.0 / HEAD_DIM)
        outs.append(xc * lax.rsqrt(ms + NORM_EPS))
    return jnp.concatenate(outs, axis=1) * gain


def _inproj_kernel(x_ref, g_ref, sh_ref, sc_ref, w_ref, qg_ref, kg_ref, bd_ref,
                   pr_ref, q_ref, k_ref, v_ref, pg_ref):
    h = _rms_mod(x_ref[0], g_ref[...], sh_ref[0], sc_ref[0]).astype(bf16)
    c1, c2 = RWKV_COLS, RWKV_COLS + ATT_COLS
    pr_ref[0] = jnp.dot(h, w_ref[:, :c1], preferred_element_type=f32)
    pa = jnp.dot(h, w_ref[:, c1:c2], preferred_element_type=f32)
    ones_bd = bd_ref[...]
    q_ref[0] = _head_norm(pa[:, :WIDTH], qg_ref[...], ones_bd).astype(bf16)
    k_ref[0] = _head_norm(pa[:, WIDTH:2 * WIDTH], kg_ref[...], ones_bd).astype(bf16)
    v_ref[0] = pa[:, 2 * WIDTH:].astype(bf16)
    pg_ref[0] = _sigmoid(jnp.dot(h, w_ref[:, c2:], preferred_element_type=f32))


def _inproj(x, gain, shift, scale, w_bf, q_g, k_g):
    bn, s, d = x.shape
    tm = min(INPROJ_ROWS, s)
    n = w_bf.shape[1]
    row = lambda b, i: (b, i, 0)
    per_b = lambda b, i: (b, 0, 0)
    const = lambda b, i: (0, 0)
    per_head = lambda g, scale: (jnp.tile(g.astype(f32), N_HEADS) * scale).reshape(1, WIDTH)
    lane_head = np.arange(128) // HEAD_DIM
    ones_bd = jnp.asarray(lane_head[:, None] == lane_head[None, :], bf16)
    return pl.pallas_call(
        _inproj_kernel,
        grid=(bn, s // tm),
        in_specs=[pl.BlockSpec((1, tm, d), row),
                  pl.BlockSpec((1, d), const),
                  pl.BlockSpec((1, 1, d), per_b),
                  pl.BlockSpec((1, 1, d), per_b),
                  pl.BlockSpec((d, n), const),
                  pl.BlockSpec((1, WIDTH), const),
                  pl.BlockSpec((1, WIDTH), const),
                  pl.BlockSpec((128, 128), const)],
        out_specs=[pl.BlockSpec((1, tm, RWKV_COLS), row),
                   pl.BlockSpec((1, tm, WIDTH), row),
                   pl.BlockSpec((1, tm, WIDTH), row),
                   pl.BlockSpec((1, tm, WIDTH), row),
                   pl.BlockSpec((1, tm, GATE_COLS), row)],
        out_shape=[jax.ShapeDtypeStruct((bn, s, RWKV_COLS), f32),
                   jax.ShapeDtypeStruct((bn, s, WIDTH), bf16),
                   jax.ShapeDtypeStruct((bn, s, WIDTH), bf16),
                   jax.ShapeDtypeStruct((bn, s, WIDTH), bf16),
                   jax.ShapeDtypeStruct((bn, s, GATE_COLS), f32)],
        compiler_params=_cparams(("parallel", "parallel")),
        name="inproj",
    )(x, gain.reshape(1, d), shift.reshape(bn, 1, d), scale.reshape(bn, 1, d), w_bf,
      per_head(q_g, HEAD_DIM ** -0.5), per_head(k_g, 1.0), ones_bd)


def _split_bf16(x):
    hi = x.astype(bf16)
    return hi, (x - hi.astype(f32)).astype(bf16)


def _mm_split(x, w2):
    rows, n = x.shape[0], w2.shape[1] // 2
    pr = jnp.dot(jnp.concatenate(_split_bf16(x), axis=0), w2, preferred_element_type=f32)
    return (pr[:rows, :n] + pr[rows:, :n]) + (pr[:rows, n:] + pr[rows:, n:])


def _bmm(a, b):
    return jnp.dot(a, b, preferred_element_type=f32)


def _bmm_nt(a, b):
    return lax.dot_general(a, b, (((1,), (1,)), ((), ())), preferred_element_type=f32)


def _bmm_tn(a, b):
    return lax.dot_general(a, b, (((0,), (0,)), ((), ())), preferred_element_type=f32)


def _rwkv_kernel(p_ref, mu_ref, w0_ref, wup_ref, a0_ref, aup_ref, gup_ref, kk_ref, ka_ref,
                 rk_ref, lg_ref, lb_ref, o_ref, last_ref, st_ref):
    C = RWKV_CHUNK
    c_idx = pl.program_id(1)

    @pl.when(c_idx == 0)
    def _():
        last_ref[...] = jnp.zeros_like(last_ref)
        st_ref[...] = jnp.zeros_like(st_ref)

    NB = p_ref.shape[0]
    R = NB * C
    rows_of = [slice(b * C, (b + 1) * C) for b in range(NB)]
    pc = p_ref[...].reshape(R, RWKV_COLS)
    row_full = lax.broadcasted_iota(jnp.int32, pc.shape, 0)
    prev = pltpu.roll(pc, 1, axis=0)
    for b in range(NB):
        prev = jnp.where(row_full == b * C, last_ref[b:b + 1, :], prev)
        last_ref[b:b + 1, :] = pc[(b + 1) * C - 1:(b + 1) * C, :]
    ps = pc + (prev - pc) * mu_ref[...]

    o1, o2, o3 = WIDTH, 2 * WIDTH, 3 * WIDTH
    o4, o5 = o3 + DECAY_LORA, o3 + DECAY_LORA + ICLR_LORA
    r, k, v = ps[:, :o1], ps[:, o1:o2], ps[:, o2:o3]
    wd, ad, gd = ps[:, o3:o4], ps[:, o4:o5], ps[:, o5:]

    wraw = w0_ref[...] + _mm_split(jnp.tanh(wd), wup_ref[...])
    z = -wraw
    softplus = jnp.maximum(z, 0.0) + jnp.log(1.0 + jnp.exp(-jnp.abs(z)))
    logdec = -jnp.exp(-softplus - 0.5)
    a = _sigmoid(a0_ref[...] + _mm_split(ad, aup_ref[...]))
    g = _mm_split(_sigmoid(gd), gup_ref[...])
    kkr = k * kk_ref[...]
    k2 = k * (1.0 + (a - 1.0) * ka_ref[...])
    rkk = r * k2 * rk_ref[...]

    ri = lax.broadcasted_iota(jnp.int32, (R, R), 0)
    ci = lax.broadcasted_iota(jnp.int32, (R, R), 1)
    tri = ((ri >= ci) & (ri // C == ci // C)).astype(bf16)
    ld_hi = logdec.astype(bf16)
    ld_r = logdec - ld_hi.astype(f32)
    ld_mid = ld_r.astype(bf16)
    ld_lo = (ld_r - ld_mid.astype(f32)).astype(bf16)
    cum3 = _bmm(tri, jnp.concatenate([ld_hi, ld_mid, ld_lo], axis=1))
    cum = cum3[:, :WIDTH] + cum3[:, WIDTH:2 * WIDTH] + cum3[:, 2 * WIDTH:]
    e_pos = jnp.exp(cum)
    e_prev = jnp.exp(cum - logdec)
    e_neg = jnp.exp(-cum)
    cum_ends = [cum[(b + 1) * C - 1:(b + 1) * C, :] for b in range(NB)]
    e_end = jnp.exp(jnp.concatenate([jnp.broadcast_to(ce, (C, WIDTH)) for ce in cum_ends], axis=0)
                    - cum)
    g_end = [jnp.exp(ce) for ce in cum_ends]

    C2 = 2 * C
    lane1 = lax.broadcasted_iota(jnp.int32, (1, 128), 1)
    keep_lo = (lane1 < HEAD_DIM).astype(bf16)
    keep_hi = (lane1 >= HEAD_DIM).astype(bf16)

    def bd(x):
        return jnp.concatenate([x * keep_lo, x * keep_hi], axis=0)

    r2 = lax.broadcasted_iota(jnp.int32, (C2, C2), 0)
    c2 = lax.broadcasted_iota(jnp.int32, (C2, C2), 1)
    ones_bd = ((r2 // HEAD_DIM) == (c2 // HEAD_DIM)).astype(bf16)
    eye = (r2 == c2).astype(f32)
    rr, cc = r2 % C, c2 % C
    strict = rr > cc
    blk16 = (rr // 16) == (cc // 16)
    blk32 = (rr // 32) == (cc // 32)
    m_diag16 = strict & blk16
    m_off32 = strict & blk32 & jnp.logical_not(blk16)
    m_off64 = strict & jnp.logical_not(blk32)
    r4 = lax.broadcasted_iota(jnp.int32, (C2, 2 * C2), 0) % C
    c4 = lax.broadcasted_iota(jnp.int32, (C2, 2 * C2), 1) % C
    strict2 = r4 > c4
    incl2 = r4 >= c4
    zeros_bd = jnp.zeros((C2, C2), bf16)

    P = range(N_HEADS // 2)
    ps_ = [slice(p * 128, (p + 1) * 128) for p in P]

    def head_sums(x):
        return jnp.concatenate([_bmm(x[:, s], ones_bd) for s in ps_], axis=1)

    kkn = kkr * lax.rsqrt(jnp.maximum(head_sums((kkr * kkr).astype(bf16)), 1e-24))
    bh = kkn * a
    at_w = (-kkn * e_prev).astype(bf16)
    rt_w = (r * e_pos).astype(bf16)
    bt_w = (bh * e_neg).astype(bf16)
    kt_w = (k2 * e_neg).astype(bf16)
    bhat_w = (bh * e_end).astype(bf16)
    khat_w = (k2 * e_end).astype(bf16)
    v_w = v.astype(bf16)

    units = [(b, p) for b in range(NB) for p in P]
    P = range(len(units))

    def sub(x, q):
        return x[rows_of[units[q][0]], ps_[units[q][1]]]

    at = [bd(sub(at_w, q)) for q in P]
    rt = [bd(sub(rt_w, q)) for q in P]
    vb = [bd(sub(v_w, q)) for q in P]
    smat = [_bmm_nt(jnp.concatenate([at[p], rt[p]], axis=0),
                    jnp.concatenate([bd(sub(bt_w, p)), bd(sub(kt_w, p))], axis=0)) for p in P]
    n_aak = [jnp.where(strict2, smat[p][:C2], 0.0) for p in P]
    arb_ark = [jnp.where(incl2, smat[p][C2:], 0.0).astype(bf16) for p in P]
    nmat = [n_aak[p][:, :C2] for p in P]

    pw = [jnp.where(m_diag16, nmat[p], 0.0) for p in P]
    tinv = [eye + pw[p] for p in P]
    for _ in range(3):
        pwb = [pw[p].astype(bf16) for p in P]
        pw = [_bmm(pwb[p], pwb[p]) for p in P]
        tinv = [tinv[p] + _bmm(tinv[p].astype(bf16), pw[p].astype(bf16)) for p in P]
    for m_off in (m_off32, m_off64):
        tb = [tinv[p].astype(bf16) for p in P]
        mid = [_bmm(jnp.where(m_off, nmat[p], 0.0).astype(bf16), tb[p]).astype(bf16) for p in P]
        tinv = [tinv[p] + _bmm(tb[p], mid[p]) for p in P]
    tb = [tinv[p].astype(bf16) for p in P]

    aakv = [_bmm(n_aak[p][:, C2:].astype(bf16), vb[p]).astype(bf16) for p in P]
    m1u0 = [_bmm(tb[p], jnp.concatenate([at[p], aakv[p]], axis=1)).astype(bf16) for p in P]
    rhs = [jnp.concatenate([m1u0[p], jnp.concatenate([zeros_bd, vb[p]], axis=1)], axis=0)
           for p in P]
    my = [_bmm(arb_ark[p], rhs[p]) for p in P]
    gd_ = [_bmm_tn(jnp.concatenate([bd(sub(bhat_w, p)), bd(sub(khat_w, p))], axis=0), rhs[p])
           for p in P]
    m2 = [(rt[p].astype(f32) + my[p][:, :C2]).astype(bf16) for p in P]
    gmat = [(eye * g_end[units[p][0]][:, ps_[units[p][1]]] + gd_[p][:, :C2]).astype(bf16)
            for p in P]
    upd = [_bmm(jnp.concatenate([m2[p], gmat[p]], axis=0), st_ref[p].astype(bf16)) for p in P]

    lane_c = lax.broadcasted_iota(jnp.int32, (C, 128), 1)
    ys = []
    for p in P:
        st_ref[p] = upd[p][C2:] + gd_[p][:, C2:]
        y_bd = upd[p][:C2] + my[p][:, C2:]
        ys.append(jnp.where(lane_c < HEAD_DIM, y_bd[:C], y_bd[C:]))
    n_pairs = len(ps_)
    y = jnp.concatenate([jnp.concatenate(ys[b * n_pairs:(b + 1) * n_pairs], axis=1)
                         for b in range(NB)], axis=0)

    y_hi = y.astype(bf16)
    y_lo = (y - y_hi.astype(f32)).astype(bf16)
    sums = head_sums(jnp.concatenate([y_hi, y_lo, rkk.astype(bf16)], axis=0))
    yc = y - (sums[:R] + sums[R:2 * R]) * (1.0 / HEAD_DIM)
    var = head_sums((yc * yc).astype(bf16)) * (1.0 / HEAD_DIM)
    yn = yc * lax.rsqrt(var + GN_EPS) * lg_ref[...] + lb_ref[...]
    o_ref[...] = ((yn + sums[2 * R:] * v) * g).reshape(NB, C, WIDTH)


def _rwkv(p_rwkv, mu, w0, w_up, a0, a_up, g_up, k_k, k_a, r_k, lnx_g, lnx_b):
    bn, s, _ = p_rwkv.shape
    C = RWKV_CHUNK
    vec = lambda t: t.reshape(1, -1)
    const = lambda b, c: (0, 0)
    full = lambda t: pl.BlockSpec(t.shape, const)
    hi_lo = lambda w: jnp.concatenate(_split_bf16(w.astype(f32)), axis=1)
    args = [vec(mu), vec(w0), hi_lo(w_up), vec(a0), hi_lo(a_up), hi_lo(g_up), vec(k_k), vec(k_a),
            vec(r_k), vec(lnx_g), vec(lnx_b)]
    nb = RWKV_BATCH_ROWS if bn % RWKV_BATCH_ROWS == 0 else 1
    return pl.pallas_call(
        _rwkv_kernel,
        grid=(bn // nb, s // C),
        in_specs=[pl.BlockSpec((nb, C, RWKV_COLS), lambda b, c: (b, c, 0))] + [full(t) for t in args],
        out_specs=pl.BlockSpec((nb, C, WIDTH), lambda b, c: (b, c, 0)),
        out_shape=jax.ShapeDtypeStruct((bn, s, WIDTH), f32),
        scratch_shapes=[pltpu.VMEM((nb, RWKV_COLS), f32),
                        pltpu.VMEM((nb * N_HEADS // 2, 2 * HEAD_DIM, 2 * HEAD_DIM), f32)],
        compiler_params=_cparams(("parallel", "arbitrary")),
        name="rwkv",
    )(p_rwkv, *args)


def _attn_kernel(q_ref, k_ref, v_ref, bias_ref, o_ref):
    TQ = ATT_QROWS
    t = pl.program_id(1)
    g0 = pl.multiple_of(jnp.maximum(t - PREV_CHUNKS * CHUNK // TQ, 0) * TQ, TQ)
    q = q_ref[0]
    lane = lax.broadcasted_iota(jnp.int32, (1, 128), 1)
    keep_lo = (lane < HEAD_DIM).astype(bf16)
    keep_hi = (lane >= HEAD_DIM).astype(bf16)
    lane_f = lax.broadcasted_iota(jnp.int32, (TQ, 128), 1)
    pairs = range(N_HEADS // 2)

    scores = []
    for p in pairs:
        qp = q[:, p * 128:(p + 1) * 128]
        qs = jnp.concatenate([qp * keep_lo, qp * keep_hi], axis=0)
        kp = k_ref[0, pl.ds(g0, ATT_WINDOW), p * 128:(p + 1) * 128]
        s = lax.dot_general(qs, kp, (((1,), (1,)), ((), ())), preferred_element_type=f32)
        scores.append(s + jnp.concatenate([bias_ref[0, 2 * p], bias_ref[0, 2 * p + 1]], axis=0))
    probs, denoms = [], []
    for p in pairs:
        m = jnp.max(scores[p], axis=-1, keepdims=True)
        e = jnp.exp(scores[p] - m)
        denoms.append(jnp.sum(e, axis=-1, keepdims=True))
        probs.append(e.astype(bf16))
    outs = []
    for p in pairs:
        vp = v_ref[0, pl.ds(g0, ATT_WINDOW), p * 128:(p + 1) * 128]
        o2 = jnp.dot(probs[p], vp, preferred_element_type=f32) / denoms[p]
        outs.append(jnp.where(lane_f < HEAD_DIM, o2[:TQ], o2[TQ:]))
    o_ref[0] = jnp.concatenate(outs, axis=-1)


def _attn_bias_tables(rel_bias):
    TQ, W = ATT_QROWS, ATT_WINDOW
    n_var = PREV_CHUNKS * CHUNK // TQ + 1
    n_h = rel_bias.shape[0]
    rb = rel_bias.astype(f32)
    ext = jnp.concatenate([jnp.broadcast_to(rb[:, :1], (n_h, W - REL_CLIP)), rb,
                           jnp.broadcast_to(rb[:, -1:], (n_h, W - REL_CLIP - 1))], axis=1)
    width = W + (n_var - 1) * TQ
    toep = jnp.stack([ext[:, TQ - qi: TQ - qi + width] for qi in range(TQ)], axis=1)
    qc = np.arange(TQ)[:, None] // CHUNK
    kc = np.arange(W)[None, :] // CHUNK
    tables = []
    for v in range(n_var):
        off = (n_var - 1 - v) * TQ
        rel_chunk = kc - qc - v * (TQ // CHUNK)
        visible = (rel_chunk <= 0) & (rel_chunk >= -PREV_CHUNKS)
        tables.append(jnp.where(jnp.asarray(visible)[None], toep[:, :, off: off + W], NEG_INF))
    return jnp.stack(tables, axis=0)


def _attn(q, k, v, rel_bias):
    bn, s, _ = q.shape
    TQ = ATT_QROWS
    bias = _attn_bias_tables(rel_bias)
    n_var = bias.shape[0]
    tile = pl.BlockSpec((1, TQ, WIDTH), lambda b, t: (b, t, 0))
    whole = pl.BlockSpec((1, s, WIDTH), lambda b, t: (b, 0, 0))
    return pl.pallas_call(
        _attn_kernel,
        grid=(bn, s // TQ),
        in_specs=[tile, whole, whole,
                  pl.BlockSpec((1, N_HEADS, TQ, ATT_WINDOW),
                               lambda b, t: (jnp.minimum(t, n_var - 1), 0, 0, 0))],
        out_specs=tile,
        out_shape=jax.ShapeDtypeStruct((bn, s, WIDTH), f32),
        compiler_params=_cparams(("parallel", "arbitrary")),
        name="attn",
    )(q, k, v, bias)


def _store_tile_rows(ref, x):
    rows = x.shape[0]
    for j in range(ROW_TILES):
        ref[pl.ds(j, rows, stride=ROW_TILES), :] = x[:, j * 128:(j + 1) * 128]


def _load_tile_rows(ref, rows):
    return jnp.concatenate(
        [ref[pl.ds(j, rows, stride=ROW_TILES), :] for j in range(ROW_TILES)], axis=1)


def _route(logits, tri, counts):
    lane = lax.broadcasted_iota(jnp.int32, logits.shape, 1).astype(f32)
    big = float(ROUTE_LANES)
    is_c = lane < N_GROUPS
    cl = jnp.where(is_c, logits, NEG_INF)
    cm = jnp.max(cl, axis=-1, keepdims=True)
    grp = jnp.min(jnp.where(cl == cm, lane, big), axis=-1, keepdims=True)
    p_grp = 1.0 / jnp.sum(jnp.where(is_c, jnp.exp(cl - cm), 0.0), axis=-1, keepdims=True)
    lo = N_GROUPS + grp * EXPERTS_PER_GROUP
    in_g = (lane >= lo) & (lane < lo + EXPERTS_PER_GROUP)
    fl = jnp.where(in_g, logits, NEG_INF)
    f1 = jnp.max(fl, axis=-1, keepdims=True)
    i1 = jnp.min(jnp.where(in_g & (fl == f1), lane, big), axis=-1, keepdims=True)
    fl2 = jnp.where(lane == i1, NEG_INF, fl)
    in_g2 = in_g & (lane != i1)
    f2 = jnp.max(fl2, axis=-1, keepdims=True)
    i2 = jnp.min(jnp.where(in_g2 & (fl2 == f2), lane, big), axis=-1, keepdims=True)
    e2 = jnp.exp(f2 - f1)
    w1 = p_grp / (1.0 + e2)
    w2 = p_grp * e2 / (1.0 + e2)

    hot1 = (lane == i1).astype(f32)
    hot2 = (lane == i2).astype(f32)
    pre1 = jnp.dot(tri, hot1.astype(bf16), preferred_element_type=f32)
    pre2 = jnp.dot(tri, hot2.astype(bf16), preferred_element_type=f32)
    tot1 = jnp.sum(hot1, axis=0, keepdims=True)
    tot2 = jnp.sum(hot2, axis=0, keepdims=True)
    rank1 = jnp.sum(hot1 * (pre1 + counts), axis=-1, keepdims=True)
    rank2 = jnp.sum(hot2 * (pre2 + counts + tot1), axis=-1, keepdims=True)

    out = jnp.where(lane == 0, w1, 0.0)
    out = jnp.where(lane == 1, w2, out)
    out = jnp.where(lane == 2, i1 - N_GROUPS, out)
    out = jnp.where(lane == 3, i2 - N_GROUPS, out)
    out = jnp.where(lane == 4, rank1, out)
    out = jnp.where(lane == 5, rank2, out)
    return out, counts + tot1 + tot2


def _merge_kernel(yr_ref, ya_ref, gt_ref, x_ref, g1_ref, sh_ref, sc_ref, ng_ref, wbr_ref,
                  wba_ref, wout_ref, wrt_ref, brt_ref, tri_ref, x1_ref, h2_ref, route_ref,
                  cnt_ref):
    @pl.when((pl.program_id(0) == 0) & (pl.program_id(1) == 0))
    def _():
        cnt_ref[...] = jnp.zeros_like(cnt_ref)

    part = tri_ref.shape[0]
    counts = cnt_ref[...]
    for i in range(x_ref.shape[1] // part):
        rs = pl.ds(i * part, part)
        t1 = jnp.dot(yr_ref[0, rs, :].astype(bf16), wbr_ref[...], preferred_element_type=f32)
        t2 = jnp.dot(ya_ref[0, rs, :].astype(bf16), wba_ref[...], preferred_element_type=f32)
        gt = gt_ref[0, rs, :]
        m = gt[:, :D_MODEL] * t1 + gt[:, D_MODEL:] * t2
        mixed = jnp.dot(m.astype(bf16), wout_ref[...], preferred_element_type=f32)
        x1 = x_ref[0, rs, :] + g1_ref[0] * mixed
        x1_ref[0, rs, :] = x1
        h2 = _rms_mod(x1, ng_ref[...], sh_ref[0], sc_ref[0])
        _store_tile_rows(h2_ref.at[0, pl.ds(i * part * ROW_TILES, part * ROW_TILES)], h2)
        h2_hi = h2.astype(bf16)
        h2_lo = (h2 - h2_hi.astype(f32)).astype(bf16)
        pa = jnp.dot(h2_hi, wrt_ref[...], preferred_element_type=f32)
        pb = jnp.dot(h2_lo, wrt_ref[...], preferred_element_type=f32)
        logits = ((pa[:, :ROUTE_LANES] + pb[:, :ROUTE_LANES])
                  + (pa[:, ROUTE_LANES:] + pb[:, ROUTE_LANES:]) + brt_ref[...])
        route, counts = _route(logits, tri_ref[...], counts)
        route_ref[0, rs, :] = route
    cnt_ref[...] = counts


def _merge(y_r, y_a, gates, x, g1, sh2, sc2, norm2_g, w_br, w_ba, w_out, w_rt, b_rt):
    bn, s, d = x.shape
    tm = min(MERGE_ROWS, s)
    row = lambda b, i: (b, i, 0)
    per_b = lambda b, i: (b, 0, 0)
    const = lambda b, i: (0, 0)
    full = lambda t: pl.BlockSpec(t.shape, const)
    part = min(MERGE_PART, tm)
    tri = jnp.asarray(np.tril(np.ones((part, part), np.float32), -1), bf16)
    return pl.pallas_call(
        _merge_kernel,
        grid=(bn, s // tm),
        in_specs=[pl.BlockSpec((1, tm, WIDTH), row),
                  pl.BlockSpec((1, tm, WIDTH), row),
                  pl.BlockSpec((1, tm, GATE_COLS), row),
                  pl.BlockSpec((1, tm, d), row),
                  pl.BlockSpec((1, 1, d), per_b),
                  pl.BlockSpec((1, 1, d), per_b),
                  pl.BlockSpec((1, 1, d), per_b),
                  pl.BlockSpec((1, d), const),
                  full(w_br), full(w_ba), full(w_out), full(w_rt), full(b_rt), full(tri)],
        out_specs=[pl.BlockSpec((1, tm, d), row),
                   pl.BlockSpec((1, tm * ROW_TILES, 128), row),
                   pl.BlockSpec((1, tm, ROUTE_LANES), row),
                   pl.BlockSpec((1, ROUTE_LANES), const)],
        out_shape=[jax.ShapeDtypeStruct((bn, s, d), f32),
                   jax.ShapeDtypeStruct((bn, s * ROW_TILES, 128), f32),
                   jax.ShapeDtypeStruct((bn, s, ROUTE_LANES), f32),
                   jax.ShapeDtypeStruct((1, ROUTE_LANES), f32)],
        compiler_params=_cparams(("arbitrary", "arbitrary")),
        name="merge",
    )(y_r, y_a, gates, x, g1.reshape(bn, 1, d), sh2.reshape(bn, 1, d), sc2.reshape(bn, 1, d),
      norm2_g.reshape(1, d), w_br, w_ba, w_out, w_rt, b_rt, tri)


def _row_copy(src, src_row, dst, dst_row, sem):
    return pltpu.make_async_copy(src.at[src_row], dst.at[pl.ds(dst_row * ROW_TILES, ROW_TILES)],
                                 sem)


def _moe_kernel(tok_ref, be_ref, j0_ref, nv_ref, h2_hbm, wg_ref, wu_ref, wd_ref, o_ref, xbuf, sem):
    del be_ref
    i = pl.program_id(0)
    last = pl.num_programs(0) - 1

    def chunks(blk, fn):
        for c in range(MOE_ROWS // MOE_CHUNK):
            @pl.when(c * MOE_CHUNK < nv_ref[blk])
            def _(c=c):
                for r in range(c * MOE_CHUNK, (c + 1) * MOE_CHUNK):
                    fn(r)

    def issue(blk, slot):
        j0 = j0_ref[blk]
        chunks(blk, lambda r: _row_copy(h2_hbm, tok_ref[j0 + r], xbuf.at[slot], r,
                                        sem.at[slot]).start(priority=r % 2))

    def wait(blk, slot):
        chunks(blk, lambda r: _row_copy(h2_hbm, 0, xbuf.at[slot], r, sem.at[slot]).wait())

    @pl.when(i == 0)
    def _():
        xbuf[...] = jnp.zeros_like(xbuf)
        issue(0, 0)

    nxt = jnp.minimum(i + 1, last)
    for slot in range(2):
        @pl.when(i % 2 == slot)
        def _(slot=slot):
            wait(i, slot)
            issue(nxt, 1 - slot)

            @pl.when(nv_ref[i] > 0)
            def _():
                xb = _load_tile_rows(xbuf.at[slot], MOE_ROWS).astype(bf16)
                gate = jnp.dot(xb, wg_ref[0].astype(bf16), preferred_element_type=f32)
                up = jnp.dot(xb, wu_ref[0].astype(bf16), preferred_element_type=f32)
                hid = gate * _sigmoid(gate) * up
                _store_tile_rows(
                    o_ref, jnp.dot(hid.astype(bf16), wd_ref[0].astype(bf16),
                                   preferred_element_type=f32))

            @pl.when(nv_ref[i] == 0)
            def _():
                o_ref[...] = jnp.zeros_like(o_ref)

            @pl.when(i == last)
            def _():
                wait(nxt, 1 - slot)


def _moe(h2_tiles, tok_sorted, block_e, block_j0, block_nv, wg, wu, wd):
    d = D_MODEL
    n_blocks = block_e.shape[0]
    blk_rows = MOE_ROWS * ROW_TILES
    by_expert = lambda i, tok, be, j0, nv: (be[i], 0, 0)
    grid_spec = pltpu.PrefetchScalarGridSpec(
        num_scalar_prefetch=4,
        grid=(n_blocks,),
        in_specs=[pl.BlockSpec(memory_space=pl.ANY),
                  pl.BlockSpec((1, d, D_EXPERT), by_expert),
                  pl.BlockSpec((1, d, D_EXPERT), by_expert),
                  pl.BlockSpec((1, D_EXPERT, d), by_expert)],
        out_specs=pl.BlockSpec((blk_rows, 128), lambda i, tok, be, j0, nv: (i, 0)),
        scratch_shapes=[pltpu.VMEM((2, blk_rows, 128), f32), pltpu.SemaphoreType.DMA((2,))],
    )
    yb = pl.pallas_call(
        _moe_kernel,
        grid_spec=grid_spec,
        out_shape=jax.ShapeDtypeStruct((n_blocks * blk_rows, 128), f32),
        compiler_params=_cparams(("arbitrary",)),
        name="moe",
    )(tok_sorted, block_e, block_j0, block_nv, h2_tiles, wg, wu, wd)
    return yb.reshape(n_blocks * MOE_ROWS, ROW_TILES, 128)


def _combine_kernel(d0_ref, d1_ref, yb_hbm, x1_ref, rt_ref, g2_ref, o_ref, buf, sem):
    tc = buf.shape[2] // ROW_TILES
    step = pl.program_id(0) * pl.num_programs(1) + pl.program_id(1)
    n_steps = pl.num_programs(0) * pl.num_programs(1)

    def issue(for_step, slot):
        base = for_step * tc
        for r in range(tc):
            _row_copy(yb_hbm, d0_ref[base + r], buf.at[slot, 0], r, sem.at[slot]).start(priority=0)
            _row_copy(yb_hbm, d1_ref[base + r], buf.at[slot, 1], r, sem.at[slot]).start(priority=1)

    def wait(slot):
        for r in range(tc):
            _row_copy(yb_hbm, 0, buf.at[slot, 0], r, sem.at[slot]).wait()
            _row_copy(yb_hbm, 0, buf.at[slot, 1], r, sem.at[slot]).wait()

    @pl.when(step == 0)
    def _():
        issue(step, 0)

    for slot in range(2):
        @pl.when(step % 2 == slot)
        def _(slot=slot):
            @pl.when(step + 1 < n_steps)
            def _():
                issue(step + 1, 1 - slot)

            wait(slot)
            rt = rt_ref[0]
            moe = (rt[:, 0:1] * _load_tile_rows(buf.at[slot, 0], tc)
                   + rt[:, 1:2] * _load_tile_rows(buf.at[slot, 1], tc))
            o_ref[0] = x1_ref[0] + g2_ref[0] * moe


def _combine(dest0, dest1, yb, x1, route, g2):
    bn, s, d = x1.shape
    tc = min(COMBINE_ROWS, s)
    row = lambda b, i, d0, d1: (b, i, 0)
    grid_spec = pltpu.PrefetchScalarGridSpec(
        num_scalar_prefetch=2,
        grid=(bn, s // tc),
        in_specs=[pl.BlockSpec(memory_space=pl.ANY),
                  pl.BlockSpec((1, tc, d), row),
                  pl.BlockSpec((1, tc, ROUTE_LANES), row),
                  pl.BlockSpec((1, 1, d), lambda b, i, d0, d1: (b, 0, 0))],
        out_specs=pl.BlockSpec((1, tc, d), row),
        scratch_shapes=[pltpu.VMEM((2, 2, tc * ROW_TILES, 128), f32),
                        pltpu.SemaphoreType.DMA((2,))],
    )
    return pl.pallas_call(
        _combine_kernel,
        grid_spec=grid_spec,
        out_shape=jax.ShapeDtypeStruct((bn, s, d), f32),
        compiler_params=_cparams(("arbitrary", "arbitrary")),
        name="combine",
    )(dest0, dest1, yb, x1, route, g2.reshape(bn, 1, d))


def _dispatch_plan(route, counts):
    n_tok = route.shape[0]
    counts = counts.astype(jnp.int32)
    padded = (counts + MOE_ROWS - 1) // MOE_ROWS * MOE_ROWS
    pad_ends = jnp.cumsum(padded)
    pad_starts = pad_ends - padded
    starts = jnp.cumsum(counts) - counts
    n_blocks = (2 * n_tok + N_EXPERTS * MOE_ROWS) // MOE_ROWS
    eid = route[:, 2:4].astype(jnp.int32)
    rank = route[:, 4:6].astype(jnp.int32)
    hot = eid[:, :, None] == jnp.arange(N_EXPERTS, dtype=jnp.int32)
    dest = rank + jnp.sum(jnp.where(hot, pad_starts, 0), axis=-1)
    tok = jnp.broadcast_to(jnp.arange(n_tok, dtype=jnp.int32)[:, None], (n_tok, 2))
    _, tok_sorted = lax.sort((dest.reshape(-1), tok.reshape(-1)), num_keys=1)
    tok_sorted = jnp.concatenate([tok_sorted, jnp.zeros((MOE_ROWS,), jnp.int32)])
    block_row = jnp.arange(n_blocks, dtype=jnp.int32) * MOE_ROWS
    block_e = jnp.minimum(jnp.sum(pad_ends[None, :] <= block_row[:, None], axis=-1),
                          N_EXPERTS - 1).astype(jnp.int32)
    hot_b = block_e[:, None] == jnp.arange(N_EXPERTS, dtype=jnp.int32)
    block_j0 = block_row + jnp.sum(jnp.where(hot_b, starts - pad_starts, 0), axis=-1)
    block_j0 = jnp.clip(block_j0, 0, 2 * n_tok - 1).astype(jnp.int32)
    block_nv = jnp.clip(jnp.sum(jnp.where(hot_b, counts + pad_starts, 0), axis=-1) - block_row,
                        0, MOE_ROWS).astype(jnp.int32)
    return tok_sorted, block_e, block_j0, block_nv, dest[:, 0], dest[:, 1]


def kernel(x, c, w_ada, b_ada, norm1_g, w_in, rwkv_mu, rwkv_w0, rwkv_w_up, rwkv_a0, rwkv_a_up, rwkv_g_up, rwkv_k_k, rwkv_k_a, rwkv_r_k, rwkv_lnx_g, rwkv_lnx_b, attn_q_g, attn_k_g, attn_rel_bias, w_branch_rwkv, w_branch_attn, w_out, norm2_g, router_coarse_w, router_coarse_b, router_fine_w, router_fine_b, expert_w_gate, expert_w_up, expert_w_down):
    bn, s, d = x.shape
    depth = w_ada.shape[0]
    for l in range(depth):
        mod = _ada(c, w_ada[l], b_ada[l])
        sh1, sc1, g1, sh2, sc2, g2 = [mod[:, i * d:(i + 1) * d] for i in range(6)]

        p_rwkv, q, k, v, gates = _inproj(x, norm1_g[l], sh1, sc1, w_in[l].astype(bf16),
                                         attn_q_g[l], attn_k_g[l])
        y_r = _rwkv(p_rwkv, rwkv_mu[l], rwkv_w0[l], rwkv_w_up[l], rwkv_a0[l], rwkv_a_up[l],
                    rwkv_g_up[l], rwkv_k_k[l], rwkv_k_a[l], rwkv_r_k[l].reshape(-1),
                    rwkv_lnx_g[l], rwkv_lnx_b[l])
        y_a = _attn(q, k, v, attn_rel_bias[l])

        n_rt = N_GROUPS + N_EXPERTS
        w_rt = jnp.zeros((d, ROUTE_LANES), f32)
        w_rt = w_rt.at[:, :N_GROUPS].set(router_coarse_w[l]).at[:, N_GROUPS:n_rt].set(router_fine_w[l])
        w_rt_hi = w_rt.astype(bf16)
        w_rt = jnp.concatenate([w_rt_hi, (w_rt - w_rt_hi.astype(f32)).astype(bf16)], axis=1)
        b_rt = jnp.zeros((1, ROUTE_LANES), f32)
        b_rt = b_rt.at[0, :N_GROUPS].set(router_coarse_b[l]).at[0, N_GROUPS:n_rt].set(router_fine_b[l])
        x1, h2, route, counts = _merge(y_r, y_a, gates, x, g1, sh2, sc2, norm2_g[l],
                                       w_branch_rwkv[l].astype(bf16),
                                       w_branch_attn[l].astype(bf16),
                                       w_out[l].astype(bf16), w_rt, b_rt)

        tok_sorted, block_e, block_j0, block_nv, dest0, dest1 = _dispatch_plan(
            route.reshape(bn * s, ROUTE_LANES), counts[0, N_GROUPS:n_rt])
        yb = _moe(h2.reshape(bn * s, ROW_TILES, 128), tok_sorted, block_e, block_j0, block_nv,
                  expert_w_gate[l], expert_w_up[l], expert_w_down[l])
        x = _combine(dest0, dest1, yb, x1, route, g2)
    return x
```

```python
import numpy as np
import jax
import jax.numpy as jnp
from jax import lax
from jax.experimental import pallas as pl
from jax.experimental.pallas import tpu as pltpu

f32 = jnp.float32
bf16 = jnp.bfloat16
HI = lax.Precision.HIGHEST

D_MODEL = 1024
HEAD_DIM = 64
N_HEADS = 8
WIDTH = N_HEADS * HEAD_DIM
DECAY_LORA = 64
ICLR_LORA = 64
GATE_LORA = 128
RWKV_COLS = 3 * WIDTH + DECAY_LORA + ICLR_LORA + GATE_LORA
ATT_COLS = 3 * WIDTH
GATE_COLS = 2 * D_MODEL
CHUNK = 64
PREV_CHUNKS = 8
REL_CLIP = 128
N_GROUPS = 4
EXPERTS_PER_GROUP = 8
N_EXPERTS = N_GROUPS * EXPERTS_PER_GROUP
D_EXPERT = 512
NORM_EPS = 1e-6
GN_EPS = 64e-5
NEG_INF = -1e30

INPROJ_ROWS = 256
RWKV_CHUNK = 64
RWKV_BATCH_ROWS = 4
ATT_QROWS = 128
ATT_WINDOW = PREV_CHUNKS * CHUNK + ATT_QROWS
MERGE_ROWS = 512
MOE_ROWS = 512
MOE_CHUNK = 64
MOE_SLOTS = 3
COMBINE_ROWS = 128
ROUTE_LANES = 128
ROW_TILES = D_MODEL // 128
VMEM_LIMIT = 56 * 1024 * 1024


def _cparams(sem):
    return pltpu.CompilerParams(dimension_semantics=sem, vmem_limit_bytes=VMEM_LIMIT)


def _sigmoid(x):
    return 1.0 / (1.0 + jnp.exp(-x))


def _ada_kernel(c_ref, w_ref, b_ref, o_ref):
    c = c_ref[...]
    s = c * _sigmoid(c)
    o_ref[...] = jnp.dot(s, w_ref[...], precision=HI, preferred_element_type=f32) + b_ref[...]


def _ada(c, w, b):
    bn, d = c.shape
    n = w.shape[1]
    tn = 1024
    return pl.pallas_call(
        _ada_kernel,
        grid=(n // tn,),
        in_specs=[pl.BlockSpec((bn, d), lambda j: (0, 0)),
                  pl.BlockSpec((d, tn), lambda j: (0, j)),
                  pl.BlockSpec((1, tn), lambda j: (0, j))],
        out_specs=pl.BlockSpec((bn, tn), lambda j: (0, j)),
        out_shape=jax.ShapeDtypeStruct((bn, n), f32),
        compiler_params=_cparams(("arbitrary",)),
        name="ada",
    )(c, w, b.reshape(1, n))


def _rms_mod(x, gain, shift, scale):
    ms = jnp.mean(x * x, axis=-1, keepdims=True)
    h = x * lax.rsqrt(ms + NORM_EPS) * gain
    return h * (1.0 + scale) + shift


def _head_norm(x, gain, ones_bd):
    outs = []
    for c in range(WIDTH // 128):
        xc = x[:, c * 128:(c + 1) * 128]
        ms = jnp.dot((xc * xc).astype(bf16), ones_bd, preferred_element_type=f32) * (1.0 / HEAD_DIM)
        outs.append(xc * lax.rsqrt(ms + NORM_EPS))
    return jnp.concatenate(outs, axis=1) * gain


def _inproj_kernel(x_ref, g_ref, sh_ref, sc_ref, w_ref, qg_ref, kg_ref, bd_ref,
                   pr_ref, q_ref, k_ref, v_ref, pg_ref):
    h = _rms_mod(x_ref[0], g_ref[...], sh_ref[0], sc_ref[0]).astype(bf16)
    c1, c2 = RWKV_COLS, RWKV_COLS + ATT_COLS
    pr_ref[0] = jnp.dot(h, w_ref[:, :c1], preferred_element_type=f32)
    pa = jnp.dot(h, w_ref[:, c1:c2], preferred_element_type=f32)
    ones_bd = bd_ref[...]
    q_ref[0] = _head_norm(pa[:, :WIDTH], qg_ref[...], ones_bd).astype(bf16)
    k_ref[0] = _head_norm(pa[:, WIDTH:2 * WIDTH], kg_ref[...], ones_bd).astype(bf16)
    v_ref[0] = pa[:, 2 * WIDTH:].astype(bf16)
    pg_ref[0] = _sigmoid(jnp.dot(h, w_ref[:, c2:], preferred_element_type=f32))


def _inproj(x, gain, shift, scale, w_bf, q_g, k_g):
    bn, s, d = x.shape
    tm = min(INPROJ_ROWS, s)
    n = w_bf.shape[1]
    row = lambda b, i: (b, i, 0)
    per_b = lambda b, i: (b, 0, 0)
    const = lambda b, i: (0, 0)
    per_head = lambda g, scale: (jnp.tile(g.astype(f32), N_HEADS) * scale).reshape(1, WIDTH)
    lane_head = np.arange(128) // HEAD_DIM
    ones_bd = jnp.asarray(lane_head[:, None] == lane_head[None, :], bf16)
    return pl.pallas_call(
        _inproj_kernel,
        grid=(bn, s // tm),
        in_specs=[pl.BlockSpec((1, tm, d), row),
                  pl.BlockSpec((1, d), const),
                  pl.BlockSpec((1, 1, d), per_b),
                  pl.BlockSpec((1, 1, d), per_b),
                  pl.BlockSpec((d, n), const),
                  pl.BlockSpec((1, WIDTH), const),
                  pl.BlockSpec((1, WIDTH), const),
                  pl.BlockSpec((128, 128), const)],
        out_specs=[pl.BlockSpec((1, tm, RWKV_COLS), row),
                   pl.BlockSpec((1, tm, WIDTH), row),
                   pl.BlockSpec((1, tm, WIDTH), row),
                   pl.BlockSpec((1, tm, WIDTH), row),
                   pl.BlockSpec((1, tm, GATE_COLS), row)],
        out_shape=[jax.ShapeDtypeStruct((bn, s, RWKV_COLS), f32),
                   jax.ShapeDtypeStruct((bn, s, WIDTH), bf16),
                   jax.ShapeDtypeStruct((bn, s, WIDTH), bf16),
                   jax.ShapeDtypeStruct((bn, s, WIDTH), bf16),
                   jax.ShapeDtypeStruct((bn, s, GATE_COLS), f32)],
        compiler_params=_cparams(("parallel", "parallel")),
        name="inproj",
    )(x, gain.reshape(1, d), shift.reshape(bn, 1, d), scale.reshape(bn, 1, d), w_bf,
      per_head(q_g, HEAD_DIM ** -0.5), per_head(k_g, 1.0), ones_bd)


def _split_bf16(x):
    hi = x.astype(bf16)
    return hi, (x - hi.astype(f32)).astype(bf16)


def _mm_split(x, w2):
    rows, n = x.shape[0], w2.shape[1] // 2
    pr = jnp.dot(jnp.concatenate(_split_bf16(x), axis=0), w2, preferred_element_type=f32)
    return (pr[:rows, :n] + pr[rows:, :n]) + (pr[:rows, n:] + pr[rows:, n:])


def _bmm(a, b):
    return jnp.dot(a, b, preferred_element_type=f32)


def _bmm_nt(a, b):
    return lax.dot_general(a, b, (((1,), (1,)), ((), ())), preferred_element_type=f32)


def _bmm_tn(a, b):
    return lax.dot_general(a, b, (((0,), (0,)), ((), ())), preferred_element_type=f32)


def _rwkv_kernel(p_ref, mu_ref, w0_ref, wup_ref, a0_ref, aup_ref, gup_ref, kk_ref, ka_ref,
                 rk_ref, lg_ref, lb_ref, o_ref, last_ref, st_ref):
    C = RWKV_CHUNK
    c_idx = pl.program_id(1)

    @pl.when(c_idx == 0)
    def _():
        last_ref[...] = jnp.zeros_like(last_ref)
        st_ref[...] = jnp.zeros_like(st_ref)

    NB = p_ref.shape[0]
    R = NB * C
    rows_of = [slice(b * C, (b + 1) * C) for b in range(NB)]
    pc = p_ref[...].reshape(R, RWKV_COLS)
    row_full = lax.broadcasted_iota(jnp.int32, pc.shape, 0)
    prev = pltpu.roll(pc, 1, axis=0)
    for b in range(NB):
        prev = jnp.where(row_full == b * C, last_ref[b:b + 1, :], prev)
        last_ref[b:b + 1, :] = pc[(b + 1) * C - 1:(b + 1) * C, :]
    ps = pc + (prev - pc) * mu_ref[...]

    o1, o2, o3 = WIDTH, 2 * WIDTH, 3 * WIDTH
    o4, o5 = o3 + DECAY_LORA, o3 + DECAY_LORA + ICLR_LORA
    r, k, v = ps[:, :o1], ps[:, o1:o2], ps[:, o2:o3]
    wd, ad, gd = ps[:, o3:o4], ps[:, o4:o5], ps[:, o5:]

    wraw = w0_ref[...] + _mm_split(jnp.tanh(wd), wup_ref[...])
    z = -wraw
    softplus = jnp.maximum(z, 0.0) + jnp.log(1.0 + jnp.exp(-jnp.abs(z)))
    logdec = -jnp.exp(-softplus - 0.5)
    a = _sigmoid(a0_ref[...] + _mm_split(ad, aup_ref[...]))
    g = _mm_split(_sigmoid(gd), gup_ref[...])
    kkr = k * kk_ref[...]
    k2 = k * (1.0 + (a - 1.0) * ka_ref[...])
    rkk = r * k2 * rk_ref[...]

    ri = lax.broadcasted_iota(jnp.int32, (R, R), 0)
    ci = lax.broadcasted_iota(jnp.int32, (R, R), 1)
    tri = ((ri >= ci) & (ri // C == ci // C)).astype(bf16)
    ld_hi = logdec.astype(bf16)
    ld_r = logdec - ld_hi.astype(f32)
    ld_mid = ld_r.astype(bf16)
    ld_lo = (ld_r - ld_mid.astype(f32)).astype(bf16)
    cum3 = _bmm(tri, jnp.concatenate([ld_hi, ld_mid, ld_lo], axis=1))
    cum = cum3[:, :WIDTH] + cum3[:, WIDTH:2 * WIDTH] + cum3[:, 2 * WIDTH:]
    e_pos = jnp.exp(cum)
    e_prev = jnp.exp(cum - logdec)
    e_neg = jnp.exp(-cum)
    cum_ends = [cum[(b + 1) * C - 1:(b + 1) * C, :] for b in range(NB)]
    e_end = jnp.exp(jnp.concatenate([jnp.broadcast_to(ce, (C, WIDTH)) for ce in cum_ends], axis=0)
                    - cum)
    g_end = [jnp.exp(ce) for ce in cum_ends]

    C2 = 2 * C
    lane1 = lax.broadcasted_iota(jnp.int32, (1, 128), 1)
    keep_lo = (lane1 < HEAD_DIM).astype(bf16)
    keep_hi = (lane1 >= HEAD_DIM).astype(bf16)

    def bd(x):
        return jnp.concatenate([x * keep_lo, x * keep_hi], axis=0)

    r2 = lax.broadcasted_iota(jnp.int32, (C2, C2), 0)
    c2 = lax.broadcasted_iota(jnp.int32, (C2, C2), 1)
    ones_bd = ((r2 // HEAD_DIM) == (c2 // HEAD_DIM)).astype(bf16)
    eye = (r2 == c2).astype(f32)
    rr, cc = r2 % C, c2 % C
    strict = rr > cc
    blk16 = (rr // 16) == (cc // 16)
    blk32 = (rr // 32) == (cc // 32)
    m_diag16 = strict & blk16
    m_off32 = strict & blk32 & jnp.logical_not(blk16)
    m_off64 = strict & jnp.logical_not(blk32)
    r4 = lax.broadcasted_iota(jnp.int32, (C2, 2 * C2), 0) % C
    c4 = lax.broadcasted_iota(jnp.int32, (C2, 2 * C2), 1) % C
    strict2 = r4 > c4
    incl2 = r4 >= c4
    zeros_bd = jnp.zeros((C2, C2), bf16)

    P = range(N_HEADS // 2)
    ps_ = [slice(p * 128, (p + 1) * 128) for p in P]

    def head_sums(x):
        return jnp.concatenate([_bmm(x[:, s], ones_bd) for s in ps_], axis=1)

    kkn = kkr * lax.rsqrt(jnp.maximum(head_sums((kkr * kkr).astype(bf16)), 1e-24))
    bh = kkn * a
    at_w = (-kkn * e_prev).astype(bf16)
    rt_w = (r * e_pos).astype(bf16)
    bt_w = (bh * e_neg).astype(bf16)
    kt_w = (k2 * e_neg).astype(bf16)
    bhat_w = (bh * e_end).astype(bf16)
    khat_w = (k2 * e_end).astype(bf16)
    v_w = v.astype(bf16)

    units = [(b, p) for b in range(NB) for p in P]
    P = range(len(units))

    def sub(x, q):
        return x[rows_of[units[q][0]], ps_[units[q][1]]]

    at = [bd(sub(at_w, q)) for q in P]
    rt = [bd(sub(rt_w, q)) for q in P]
    vb = [bd(sub(v_w, q)) for q in P]
    smat = [_bmm_nt(jnp.concatenate([at[p], rt[p]], axis=0),
                    jnp.concatenate([bd(sub(bt_w, p)), bd(sub(kt_w, p))], axis=0)) for p in P]
    n_aak = [jnp.where(strict2, smat[p][:C2], 0.0) for p in P]
    arb_ark = [jnp.where(incl2, smat[p][C2:], 0.0).astype(bf16) for p in P]
    nmat = [n_aak[p][:, :C2] for p in P]

    pw = [jnp.where(m_diag16, nmat[p], 0.0) for p in P]
    tinv = [eye + pw[p] for p in P]
    for _ in range(3):
        pwb = [pw[p].astype(bf16) for p in P]
        pw = [_bmm(pwb[p], pwb[p]) for p in P]
        tinv = [tinv[p] + _bmm(tinv[p].astype(bf16), pw[p].astype(bf16)) for p in P]
    for m_off in (m_off32, m_off64):
        tb = [tinv[p].astype(bf16) for p in P]
        mid = [_bmm(jnp.where(m_off, nmat[p], 0.0).astype(bf16), tb[p]).astype(bf16) for p in P]
        tinv = [tinv[p] + _bmm(tb[p], mid[p]) for p in P]
    tb = [tinv[p].astype(bf16) for p in P]

    aakv = [_bmm(n_aak[p][:, C2:].astype(bf16), vb[p]).astype(bf16) for p in P]
    m1u0 = [_bmm(tb[p], jnp.concatenate([at[p], aakv[p]], axis=1)).astype(bf16) for p in P]
    rhs = [jnp.concatenate([m1u0[p], jnp.concatenate([zeros_bd, vb[p]], axis=1)], axis=0)
           for p in P]
    my = [_bmm(arb_ark[p], rhs[p]) for p in P]
    gd_ = [_bmm_tn(jnp.concatenate([bd(sub(bhat_w, p)), bd(sub(khat_w, p))], axis=0), rhs[p])
           for p in P]
    m2 = [(rt[p].astype(f32) + my[p][:, :C2]).astype(bf16) for p in P]
    gmat = [(eye * g_end[units[p][0]][:, ps_[units[p][1]]] + gd_[p][:, :C2]).astype(bf16)
            for p in P]
    upd = [_bmm(jnp.concatenate([m2[p], gmat[p]], axis=0), st_ref[p].astype(bf16)) for p in P]

    lane_c = lax.broadcasted_iota(jnp.int32, (C, 128), 1)
    ys = []
    for p in P:
        st_ref[p] = upd[p][C2:] + gd_[p][:, C2:]
        y_bd = upd[p][:C2] + my[p][:, C2:]
        ys.append(jnp.where(lane_c < HEAD_DIM, y_bd[:C], y_bd[C:]))
    n_pairs = len(ps_)
    y = jnp.concatenate([jnp.concatenate(ys[b * n_pairs:(b + 1) * n_pairs], axis=1)
                         for b in range(NB)], axis=0)

    y_hi = y.astype(bf16)
    y_lo = (y - y_hi.astype(f32)).astype(bf16)
    sums = head_sums(jnp.concatenate([y_hi, y_lo, rkk.astype(bf16)], axis=0))
    yc = y - (sums[:R] + sums[R:2 * R]) * (1.0 / HEAD_DIM)
    var = head_sums((yc * yc).astype(bf16)) * (1.0 / HEAD_DIM)
    yn = yc * lax.rsqrt(var + GN_EPS) * lg_ref[...] + lb_ref[...]
    o_ref[...] = ((yn + sums[2 * R:] * v) * g).reshape(NB, C, WIDTH)


def _rwkv(p_rwkv, mu, w0, w_up, a0, a_up, g_up, k_k, k_a, r_k, lnx_g, lnx_b):
    bn, s, _ = p_rwkv.shape
    C = RWKV_CHUNK
    vec = lambda t: t.reshape(1, -1)
    const = lambda b, c: (0, 0)
    full = lambda t: pl.BlockSpec(t.shape, const)
    hi_lo = lambda w: jnp.concatenate(_split_bf16(w.astype(f32)), axis=1)
    args = [vec(mu), vec(w0), hi_lo(w_up), vec(a0), hi_lo(a_up), hi_lo(g_up), vec(k_k), vec(k_a),
            vec(r_k), vec(lnx_g), vec(lnx_b)]
    nb = RWKV_BATCH_ROWS if bn % RWKV_BATCH_ROWS == 0 else 1
    return pl.pallas_call(
        _rwkv_kernel,
        grid=(bn // nb, s // C),
        in_specs=[pl.BlockSpec((nb, C, RWKV_COLS), lambda b, c: (b, c, 0))] + [full(t) for t in args],
        out_specs=pl.BlockSpec((nb, C, WIDTH), lambda b, c: (b, c, 0)),
        out_shape=jax.ShapeDtypeStruct((bn, s, WIDTH), f32),
        scratch_shapes=[pltpu.VMEM((nb, RWKV_COLS), f32),
                        pltpu.VMEM((nb * N_HEADS // 2, 2 * HEAD_DIM, 2 * HEAD_DIM), f32)],
        compiler_params=_cparams(("parallel", "arbitrary")),
        name="rwkv",
    )(p_rwkv, *args)


def _attn_kernel(q_ref, k_ref, v_ref, bias_ref, o_ref):
    TQ = ATT_QROWS
    t = pl.program_id(1)
    g0 = pl.multiple_of(jnp.maximum(t - PREV_CHUNKS * CHUNK // TQ, 0) * TQ, TQ)
    q = q_ref[0]
    lane = lax.broadcasted_iota(jnp.int32, (1, 128), 1)
    keep_lo = (lane < HEAD_DIM).astype(bf16)
    keep_hi = (lane >= HEAD_DIM).astype(bf16)
    lane_f = lax.broadcasted_iota(jnp.int32, (TQ, 128), 1)
    pairs = range(N_HEADS // 2)

    scores = []
    for p in pairs:
        qp = q[:, p * 128:(p + 1) * 128]
        qs = jnp.concatenate([qp * keep_lo, qp * keep_hi], axis=0)
        kp = k_ref[0, pl.ds(g0, ATT_WINDOW), p * 128:(p + 1) * 128]
        s = lax.dot_general(qs, kp, (((1,), (1,)), ((), ())), preferred_element_type=f32)
        scores.append(s + jnp.concatenate([bias_ref[0, 2 * p], bias_ref[0, 2 * p + 1]], axis=0))
    probs, denoms = [], []
    for p in pairs:
        m = jnp.max(scores[p], axis=-1, keepdims=True)
        e = jnp.exp(scores[p] - m)
        denoms.append(jnp.sum(e, axis=-1, keepdims=True))
        probs.append(e.astype(bf16))
    outs = []
    for p in pairs:
        vp = v_ref[0, pl.ds(g0, ATT_WINDOW), p * 128:(p + 1) * 128]
        o2 = jnp.dot(probs[p], vp, preferred_element_type=f32) / denoms[p]
        outs.append(jnp.where(lane_f < HEAD_DIM, o2[:TQ], o2[TQ:]))
    o_ref[0] = jnp.concatenate(outs, axis=-1)


def _attn_bias_tables(rel_bias):
    TQ, W = ATT_QROWS, ATT_WINDOW
    n_var = PREV_CHUNKS * CHUNK // TQ + 1
    n_h = rel_bias.shape[0]
    rb = rel_bias.astype(f32)
    ext = jnp.concatenate([jnp.broadcast_to(rb[:, :1], (n_h, W - REL_CLIP)), rb,
                           jnp.broadcast_to(rb[:, -1:], (n_h, W - REL_CLIP - 1))], axis=1)
    width = W + (n_var - 1) * TQ
    toep = jnp.stack([ext[:, TQ - qi: TQ - qi + width] for qi in range(TQ)], axis=1)
    qc = np.arange(TQ)[:, None] // CHUNK
    kc = np.arange(W)[None, :] // CHUNK
    tables = []
    for v in range(n_var):
        off = (n_var - 1 - v) * TQ
        rel_chunk = kc - qc - v * (TQ // CHUNK)
        visible = (rel_chunk <= 0) & (rel_chunk >= -PREV_CHUNKS)
        tables.append(jnp.where(jnp.asarray(visible)[None], toep[:, :, off: off + W], NEG_INF))
    return jnp.stack(tables, axis=0)


def _attn(q, k, v, rel_bias):
    bn, s, _ = q.shape
    TQ = ATT_QROWS
    bias = _attn_bias_tables(rel_bias)
    n_var = bias.shape[0]
    tile = pl.BlockSpec((1, TQ, WIDTH), lambda b, t: (b, t, 0))
    whole = pl.BlockSpec((1, s, WIDTH), lambda b, t: (b, 0, 0))
    return pl.pallas_call(
        _attn_kernel,
        grid=(bn, s // TQ),
        in_specs=[tile, whole, whole,
                  pl.BlockSpec((1, N_HEADS, TQ, ATT_WINDOW),
                               lambda b, t: (jnp.minimum(t, n_var - 1), 0, 0, 0))],
        out_specs=tile,
        out_shape=jax.ShapeDtypeStruct((bn, s, WIDTH), f32),
        compiler_params=_cparams(("parallel", "arbitrary")),
        name="attn",
    )(q, k, v, bias)


def _store_tile_rows(ref, x):
    rows = x.shape[0]
    for j in range(ROW_TILES):
        ref[pl.ds(j, rows, stride=ROW_TILES), :] = x[:, j * 128:(j + 1) * 128]


def _load_tile_rows(ref, rows):
    return jnp.concatenate(
        [ref[pl.ds(j, rows, stride=ROW_TILES), :] for j in range(ROW_TILES)], axis=1)


def _route(logits, tri, counts):
    lane = lax.broadcasted_iota(jnp.int32, logits.shape, 1).astype(f32)
    big = float(ROUTE_LANES)
    is_c = lane < N_GROUPS
    cl = jnp.where(is_c, logits, NEG_INF)
    cm = jnp.max(cl, axis=-1, keepdims=True)
    grp = jnp.min(jnp.where(cl == cm, lane, big), axis=-1, keepdims=True)
    p_grp = 1.0 / jnp.sum(jnp.where(is_c, jnp.exp(cl - cm), 0.0), axis=-1, keepdims=True)
    lo = N_GROUPS + grp * EXPERTS_PER_GROUP
    in_g = (lane >= lo) & (lane < lo + EXPERTS_PER_GROUP)
    fl = jnp.where(in_g, logits, NEG_INF)
    f1 = jnp.max(fl, axis=-1, keepdims=True)
    i1 = jnp.min(jnp.where(in_g & (fl == f1), lane, big), axis=-1, keepdims=True)
    fl2 = jnp.where(lane == i1, NEG_INF, fl)
    in_g2 = in_g & (lane != i1)
    f2 = jnp.max(fl2, axis=-1, keepdims=True)
    i2 = jnp.min(jnp.where(in_g2 & (fl2 == f2), lane, big), axis=-1, keepdims=True)
    e2 = jnp.exp(f2 - f1)
    w1 = p_grp / (1.0 + e2)
    w2 = p_grp * e2 / (1.0 + e2)

    hot1 = (lane == i1).astype(f32)
    hot2 = (lane == i2).astype(f32)
    pre1 = jnp.dot(tri, hot1.astype(bf16), preferred_element_type=f32)
    pre2 = jnp.dot(tri, hot2.astype(bf16), preferred_element_type=f32)
    tot1 = jnp.sum(hot1, axis=0, keepdims=True)
    tot2 = jnp.sum(hot2, axis=0, keepdims=True)
    rank1 = jnp.sum(hot1 * (pre1 + counts), axis=-1, keepdims=True)
    rank2 = jnp.sum(hot2 * (pre2 + counts + tot1), axis=-1, keepdims=True)

    out = jnp.where(lane == 0, w1, 0.0)
    out = jnp.where(lane == 1, w2, out)
    out = jnp.where(lane == 2, i1 - N_GROUPS, out)
    out = jnp.where(lane == 3, i2 - N_GROUPS, out)
    out = jnp.where(lane == 4, rank1, out)
    out = jnp.where(lane == 5, rank2, out)
    return out, counts + tot1 + tot2


def _merge_kernel(yr_ref, ya_ref, gt_ref, x_ref, g1_ref, sh_ref, sc_ref, ng_ref, wbr_ref,
                  wba_ref, wout_ref, wrt_ref, brt_ref, tri_ref, x1_ref, h2_ref, route_ref,
                  cnt_ref):
    @pl.when((pl.program_id(0) == 0) & (pl.program_id(1) == 0))
    def _():
        cnt_ref[...] = jnp.zeros_like(cnt_ref)

    t1 = jnp.dot(yr_ref[0].astype(bf16), wbr_ref[...], preferred_element_type=f32)
    t2 = jnp.dot(ya_ref[0].astype(bf16), wba_ref[...], preferred_element_type=f32)
    gt = gt_ref[0]
    m = gt[:, :D_MODEL] * t1 + gt[:, D_MODEL:] * t2
    mixed = jnp.dot(m.astype(bf16), wout_ref[...], preferred_element_type=f32)
    x1 = x_ref[0] + g1_ref[0] * mixed
    x1_ref[0] = x1
    h2 = _rms_mod(x1, ng_ref[...], sh_ref[0], sc_ref[0])
    _store_tile_rows(h2_ref.at[0], h2)
    h2_hi = h2.astype(bf16)
    h2_lo = (h2 - h2_hi.astype(f32)).astype(bf16)
    pa = jnp.dot(h2_hi, wrt_ref[...], preferred_element_type=f32)
    pb = jnp.dot(h2_lo, wrt_ref[...], preferred_element_type=f32)
    logits = ((pa[:, :ROUTE_LANES] + pb[:, :ROUTE_LANES])
              + (pa[:, ROUTE_LANES:] + pb[:, ROUTE_LANES:]) + brt_ref[...])
    route, counts = _route(logits, tri_ref[...], cnt_ref[...])
    route_ref[0] = route
    cnt_ref[...] = counts


def _merge(y_r, y_a, gates, x, g1, sh2, sc2, norm2_g, w_br, w_ba, w_out, w_rt, b_rt):
    bn, s, d = x.shape
    tm = min(MERGE_ROWS, s)
    row = lambda b, i: (b, i, 0)
    per_b = lambda b, i: (b, 0, 0)
    const = lambda b, i: (0, 0)
    full = lambda t: pl.BlockSpec(t.shape, const)
    tri = jnp.asarray(np.tril(np.ones((tm, tm), np.float32), -1), bf16)
    return pl.pallas_call(
        _merge_kernel,
        grid=(bn, s // tm),
        in_specs=[pl.BlockSpec((1, tm, WIDTH), row),
                  pl.BlockSpec((1, tm, WIDTH), row),
                  pl.BlockSpec((1, tm, GATE_COLS), row),
                  pl.BlockSpec((1, tm, d), row),
                  pl.BlockSpec((1, 1, d), per_b),
                  pl.BlockSpec((1, 1, d), per_b),
                  pl.BlockSpec((1, 1, d), per_b),
                  pl.BlockSpec((1, d), const),
                  full(w_br), full(w_ba), full(w_out), full(w_rt), full(b_rt), full(tri)],
        out_specs=[pl.BlockSpec((1, tm, d), row),
                   pl.BlockSpec((1, tm * ROW_TILES, 128), row),
                   pl.BlockSpec((1, tm, ROUTE_LANES), row),
                   pl.BlockSpec((1, ROUTE_LANES), const)],
        out_shape=[jax.ShapeDtypeStruct((bn, s, d), f32),
                   jax.ShapeDtypeStruct((bn, s * ROW_TILES, 128), f32),
                   jax.ShapeDtypeStruct((bn, s, ROUTE_LANES), f32),
                   jax.ShapeDtypeStruct((1, ROUTE_LANES), f32)],
        compiler_params=_cparams(("arbitrary", "arbitrary")),
        name="merge",
    )(y_r, y_a, gates, x, g1.reshape(bn, 1, d), sh2.reshape(bn, 1, d), sc2.reshape(bn, 1, d),
      norm2_g.reshape(1, d), w_br, w_ba, w_out, w_rt, b_rt, tri)


def _row_copy(src, src_row, dst, dst_row, sem):
    return pltpu.make_async_copy(src.at[src_row], dst.at[pl.ds(dst_row * ROW_TILES, ROW_TILES)],
                                 sem)


def _moe_kernel(tok_ref, be_ref, j0_ref, nv_ref, h2_hbm, wg_ref, wu_ref, wd_ref, o_ref, xbuf, sem):
    del be_ref
    i = pl.program_id(0)
    last = pl.num_programs(0) - 1

    def chunks(blk, fn):
        for c in range(MOE_ROWS // MOE_CHUNK):
            @pl.when(c * MOE_CHUNK < nv_ref[blk])
            def _(c=c):
                for r in range(c * MOE_CHUNK, (c + 1) * MOE_CHUNK):
                    fn(r)

    def issue(blk, slot):
        j0 = j0_ref[blk]
        chunks(blk, lambda r: _row_copy(h2_hbm, tok_ref[j0 + r], xbuf.at[slot], r,
                                        sem.at[slot]).start(priority=r % 2))

    def wait(blk, slot):
        chunks(blk, lambda r: _row_copy(h2_hbm, 0, xbuf.at[slot], r, sem.at[slot]).wait())

    n_slots = xbuf.shape[0]
    ahead = n_slots - 1

    @pl.when(i == 0)
    def _():
        xbuf[...] = jnp.zeros_like(xbuf)
        for b in range(ahead):
            @pl.when(b <= last)
            def _(b=b):
                issue(b, b)

    for slot in range(n_slots):
        @pl.when(i % n_slots == slot)
        def _(slot=slot):
            wait(i, slot)

            @pl.when(i + ahead <= last)
            def _():
                issue(i + ahead, (slot + ahead) % n_slots)

            @pl.when(nv_ref[i] > 0)
            def _():
                xb = _load_tile_rows(xbuf.at[slot], MOE_ROWS).astype(bf16)
                gate = jnp.dot(xb, wg_ref[0].astype(bf16), preferred_element_type=f32)
                up = jnp.dot(xb, wu_ref[0].astype(bf16), preferred_element_type=f32)
                hid = gate * _sigmoid(gate) * up
                _store_tile_rows(
                    o_ref, jnp.dot(hid.astype(bf16), wd_ref[0].astype(bf16),
                                   preferred_element_type=f32))

            @pl.when(nv_ref[i] == 0)
            def _():
                o_ref[...] = jnp.zeros_like(o_ref)


def _moe(h2_tiles, tok_sorted, block_e, block_j0, block_nv, wg, wu, wd):
    d = D_MODEL
    n_blocks = block_e.shape[0]
    blk_rows = MOE_ROWS * ROW_TILES
    by_expert = lambda i, tok, be, j0, nv: (be[i], 0, 0)
    grid_spec = pltpu.PrefetchScalarGridSpec(
        num_scalar_prefetch=4,
        grid=(n_blocks,),
        in_specs=[pl.BlockSpec(memory_space=pl.ANY),
                  pl.BlockSpec((1, d, D_EXPERT), by_expert),
                  pl.BlockSpec((1, d, D_EXPERT), by_expert),
                  pl.BlockSpec((1, D_EXPERT, d), by_expert)],
        out_specs=pl.BlockSpec((blk_rows, 128), lambda i, tok, be, j0, nv: (i, 0)),
        scratch_shapes=[pltpu.VMEM((MOE_SLOTS, blk_rows, 128), f32),
                        pltpu.SemaphoreType.DMA((MOE_SLOTS,))],
    )
    yb = pl.pallas_call(
        _moe_kernel,
        grid_spec=grid_spec,
        out_shape=jax.ShapeDtypeStruct((n_blocks * blk_rows, 128), f32),
        compiler_params=_cparams(("arbitrary",)),
        name="moe",
    )(tok_sorted, block_e, block_j0, block_nv, h2_tiles, wg, wu, wd)
    return yb.reshape(n_blocks * MOE_ROWS, ROW_TILES, 128)


def _combine_kernel(d0_ref, d1_ref, yb_hbm, x1_ref, rt_ref, g2_ref, o_ref, buf, sem):
    tc = buf.shape[2] // ROW_TILES
    step = pl.program_id(0) * pl.num_programs(1) + pl.program_id(1)
    n_steps = pl.num_programs(0) * pl.num_programs(1)

    def issue(for_step, slot):
        base = for_step * tc
        for r in range(tc):
            _row_copy(yb_hbm, d0_ref[base + r], buf.at[slot, 0], r, sem.at[slot]).start(priority=0)
            _row_copy(yb_hbm, d1_ref[base + r], buf.at[slot, 1], r, sem.at[slot]).start(priority=1)

    def wait(slot):
        for r in range(tc):
            _row_copy(yb_hbm, 0, buf.at[slot, 0], r, sem.at[slot]).wait()
            _row_copy(yb_hbm, 0, buf.at[slot, 1], r, sem.at[slot]).wait()

    @pl.when(step == 0)
    def _():
        issue(step, 0)

    for slot in range(2):
        @pl.when(step % 2 == slot)
        def _(slot=slot):
            @pl.when(step + 1 < n_steps)
            def _():
                issue(step + 1, 1 - slot)

            wait(slot)
            rt = rt_ref[0]
            moe = (rt[:, 0:1] * _load_tile_rows(buf.at[slot, 0], tc)
                   + rt[:, 1:2] * _load_tile_rows(buf.at[slot, 1], tc))
            o_ref[0] = x1_ref[0] + g2_ref[0] * moe


def _combine(dest0, dest1, yb, x1, route, g2):
    bn, s, d = x1.shape
    tc = min(COMBINE_ROWS, s)
    row = lambda b, i, d0, d1: (b, i, 0)
    grid_spec = pltpu.PrefetchScalarGridSpec(
        num_scalar_prefetch=2,
        grid=(bn, s // tc),
        in_specs=[pl.BlockSpec(memory_space=pl.ANY),
                  pl.BlockSpec((1, tc, d), row),
                  pl.BlockSpec((1, tc, ROUTE_LANES), row),
                  pl.BlockSpec((1, 1, d), lambda b, i, d0, d1: (b, 0, 0))],
        out_specs=pl.BlockSpec((1, tc, d), row),
        scratch_shapes=[pltpu.VMEM((2, 2, tc * ROW_TILES, 128), f32),
                        pltpu.SemaphoreType.DMA((2,))],
    )
    return pl.pallas_call(
        _combine_kernel,
        grid_spec=grid_spec,
        out_shape=jax.ShapeDtypeStruct((bn, s, d), f32),
        compiler_params=_cparams(("arbitrary", "arbitrary")),
        name="combine",
    )(dest0, dest1, yb, x1, route, g2.reshape(bn, 1, d))


def _dispatch_plan(route, counts):
    n_tok = route.shape[0]
    counts = counts.astype(jnp.int32)
    padded = (counts + MOE_ROWS - 1) // MOE_ROWS * MOE_ROWS
    pad_ends = jnp.cumsum(padded)
    pad_starts = pad_ends - padded
    starts = jnp.cumsum(counts) - counts
    n_blocks = (2 * n_tok + N_EXPERTS * MOE_ROWS) // MOE_ROWS
    eid = route[:, 2:4].astype(jnp.int32)
    rank = route[:, 4:6].astype(jnp.int32)
    hot = eid[:, :, None] == jnp.arange(N_EXPERTS, dtype=jnp.int32)
    dest = rank + jnp.sum(jnp.where(hot, pad_starts, 0), axis=-1)
    tok = jnp.broadcast_to(jnp.arange(n_tok, dtype=jnp.int32)[:, None], (n_tok, 2))
    _, tok_sorted = lax.sort((dest.reshape(-1), tok.reshape(-1)), num_keys=1)
    tok_sorted = jnp.concatenate([tok_sorted, jnp.zeros((MOE_ROWS,), jnp.int32)])
    block_row = jnp.arange(n_blocks, dtype=jnp.int32) * MOE_ROWS
    block_e = jnp.minimum(jnp.sum(pad_ends[None, :] <= block_row[:, None], axis=-1),
                          N_EXPERTS - 1).astype(jnp.int32)
    hot_b = block_e[:, None] == jnp.arange(N_EXPERTS, dtype=jnp.int32)
    block_j0 = block_row + jnp.sum(jnp.where(hot_b, starts - pad_starts, 0), axis=-1)
    block_j0 = jnp.clip(block_j0, 0, 2 * n_tok - 1).astype(jnp.int32)
    block_nv = jnp.clip(jnp.sum(jnp.where(hot_b, counts + pad_starts, 0), axis=-1) - block_row,
                        0, MOE_ROWS).astype(jnp.int32)
    return tok_sorted, block_e, block_j0, block_nv, dest[:, 0], dest[:, 1]


def kernel(x, c, w_ada, b_ada, norm1_g, w_in, rwkv_mu, rwkv_w0, rwkv_w_up, rwkv_a0, rwkv_a_up, rwkv_g_up, rwkv_k_k, rwkv_k_a, rwkv_r_k, rwkv_lnx_g, rwkv_lnx_b, attn_q_g, attn_k_g, attn_rel_bias, w_branch_rwkv, w_branch_attn, w_out, norm2_g, router_coarse_w, router_coarse_b, router_fine_w, router_fine_b, expert_w_gate, expert_w_up, expert_w_down):
    bn, s, d = x.shape
    depth = w_ada.shape[0]
    for l in range(depth):
        mod = _ada(c, w_ada[l], b_ada[l])
        sh1, sc1, g1, sh2, sc2, g2 = [mod[:, i * d:(i + 1) * d] for i in range(6)]

        p_rwkv, q, k, v, gates = _inproj(x, norm1_g[l], sh1, sc1, w_in[l].astype(bf16),
                                         attn_q_g[l], attn_k_g[l])
        y_r = _rwkv(p_rwkv, rwkv_mu[l], rwkv_w0[l], rwkv_w_up[l], rwkv_a0[l], rwkv_a_up[l],
                    rwkv_g_up[l], rwkv_k_k[l], rwkv_k_a[l], rwkv_r_k[l].reshape(-1),
                    rwkv_lnx_g[l], rwkv_lnx_b[l])
        y_a = _attn(q, k, v, attn_rel_bias[l])

        n_rt = N_GROUPS + N_EXPERTS
        w_rt = jnp.zeros((d, ROUTE_LANES), f32)
        w_rt = w_rt.at[:, :N_GROUPS].set(router_coarse_w[l]).at[:, N_GROUPS:n_rt].set(router_fine_w[l])
        w_rt_hi = w_rt.astype(bf16)
        w_rt = jnp.concatenate([w_rt_hi, (w_rt - w_rt_hi.astype(f32)).astype(bf16)], axis=1)
        b_rt = jnp.zeros((1, ROUTE_LANES), f32)
        b_rt = b_rt.at[0, :N_GROUPS].set(router_coarse_b[l]).at[0, N_GROUPS:n_rt].set(router_fine_b[l])
        x1, h2, route, counts = _merge(y_r, y_a, gates, x, g1, sh2, sc2, norm2_g[l],
                                       w_branch_rwkv[l].astype(bf16),
                                       w_branch_attn[l].astype(bf16),
                                       w_out[l].astype(bf16), w_rt, b_rt)

        tok_sorted, block_e, block_j0, block_nv, dest0, dest1 = _dispatch_plan(
            route.reshape(bn * s, ROUTE_LANES), counts[0, N_GROUPS:n_rt])
        yb = _moe(h2.reshape(bn * s, ROW_TILES, 128), tok_sorted, block_e, block_j0, block_nv,
                  expert_w_gate[l], expert_w_up[l], expert_w_down[l])
        x = _combine(dest0, dest1, yb, x1, route, g2)
    return x
```

```python
import numpy as np
import jax
import jax.numpy as jnp
from jax import lax
from jax.experimental import pallas as pl
from jax.experimental.pallas import tpu as pltpu

f32 = jnp.float32
bf16 = jnp.bfloat16
HI = lax.Precision.HIGHEST

D_MODEL = 1024
HEAD_DIM = 64
N_HEADS = 8
WIDTH = N_HEADS * HEAD_DIM
DECAY_LORA = 64
ICLR_LORA = 64
GATE_LORA = 128
RWKV_COLS = 3 * WIDTH + DECAY_LORA + ICLR_LORA + GATE_LORA
ATT_COLS = 3 * WIDTH
GATE_COLS = 2 * D_MODEL
CHUNK = 64
PREV_CHUNKS = 8
REL_CLIP = 128
N_GROUPS = 4
EXPERTS_PER_GROUP = 8
N_EXPERTS = N_GROUPS * EXPERTS_PER_GROUP
D_EXPERT = 512
NORM_EPS = 1e-6
GN_EPS = 64e-5
NEG_INF = -1e30

INPROJ_ROWS = 256
RWKV_CHUNK = 64
RWKV_BATCH_ROWS = 4
ATT_QROWS = 128
ATT_WINDOW = PREV_CHUNKS * CHUNK + ATT_QROWS
MERGE_ROWS = 512
MOE_ROWS = 512
MOE_CHUNK = 64
MOE_SLOTS = 3
COMBINE_ROWS = 128
ROUTE_LANES = 128
ROW_TILES = D_MODEL // 128
VMEM_LIMIT = 56 * 1024 * 1024


def _cparams(sem):
    return pltpu.CompilerParams(dimension_semantics=sem, vmem_limit_bytes=VMEM_LIMIT)


def _sigmoid(x):
    return 1.0 / (1.0 + jnp.exp(-x))


def _ada_kernel(c_ref, w_ref, b_ref, o_ref):
    c = c_ref[...]
    s = c * _sigmoid(c)
    o_ref[...] = jnp.dot(s, w_ref[...], precision=HI, preferred_element_type=f32) + b_ref[...]


def _ada(c, w, b):
    bn, d = c.shape
    n = w.shape[1]
    tn = 1024
    return pl.pallas_call(
        _ada_kernel,
        grid=(n // tn,),
        in_specs=[pl.BlockSpec((bn, d), lambda j: (0, 0)),
                  pl.BlockSpec((d, tn), lambda j: (0, j)),
                  pl.BlockSpec((1, tn), lambda j: (0, j))],
        out_specs=pl.BlockSpec((bn, tn), lambda j: (0, j)),
        out_shape=jax.ShapeDtypeStruct((bn, n), f32),
        compiler_params=_cparams(("arbitrary",)),
        name="ada",
    )(c, w, b.reshape(1, n))


def _rms_mod(x, gain, shift, scale):
    ms = jnp.mean(x * x, axis=-1, keepdims=True)
    h = x * lax.rsqrt(ms + NORM_EPS) * gain
    return h * (1.0 + scale) + shift


def _head_norm(x, gain, ones_bd):
    outs = []
    for c in range(WIDTH // 128):
        xc = x[:, c * 128:(c + 1) * 128]
        ms = jnp.dot((xc * xc).astype(bf16), ones_bd, preferred_element_type=f32) * (1.0 / HEAD_DIM)
        outs.append(xc * lax.rsqrt(ms + NORM_EPS))
    return jnp.concatenate(outs, axis=1) * gain


def _inproj_kernel(x_ref, g_ref, sh_ref, sc_ref, w_ref, qg_ref, kg_ref, bd_ref,
                   pr_ref, q_ref, k_ref, v_ref, pg_ref):
    h = _rms_mod(x_ref[0], g_ref[...], sh_ref[0], sc_ref[0]).astype(bf16)
    c1, c2 = RWKV_COLS, RWKV_COLS + ATT_COLS
    pr_ref[0] = jnp.dot(h, w_ref[:, :c1], preferred_element_type=f32)
    pa = jnp.dot(h, w_ref[:, c1:c2], preferred_element_type=f32)
    ones_bd = bd_ref[...]
    q_ref[0] = _head_norm(pa[:, :WIDTH], qg_ref[...], ones_bd).astype(bf16)
    k_ref[0] = _head_norm(pa[:, WIDTH:2 * WIDTH], kg_ref[...], ones_bd).astype(bf16)
    v_ref[0] = pa[:, 2 * WIDTH:].astype(bf16)
    pg_ref[0] = _sigmoid(jnp.dot(h, w_ref[:, c2:], preferred_element_type=f32))


def _inproj(x, gain, shift, scale, w_bf, q_g, k_g):
    bn, s, d = x.shape
    tm = min(INPROJ_ROWS, s)
    n = w_bf.shape[1]
    row = lambda b, i: (b, i, 0)
    per_b = lambda b, i: (b, 0, 0)
    const = lambda b, i: (0, 0)
    per_head = lambda g, scale: (jnp.tile(g.astype(f32), N_HEADS) * scale).reshape(1, WIDTH)
    lane_head = np.arange(128) // HEAD_DIM
    ones_bd = jnp.asarray(lane_head[:, None] == lane_head[None, :], bf16)
    return pl.pallas_call(
        _inproj_kernel,
        grid=(bn, s // tm),
        in_specs=[pl.BlockSpec((1, tm, d), row),
                  pl.BlockSpec((1, d), const),
                  pl.BlockSpec((1, 1, d), per_b),
                  pl.BlockSpec((1, 1, d), per_b),
                  pl.BlockSpec((d, n), const),
                  pl.BlockSpec((1, WIDTH), const),
                  pl.BlockSpec((1, WIDTH), const),
                  pl.BlockSpec((128, 128), const)],
        out_specs=[pl.BlockSpec((1, tm, RWKV_COLS), row),
                   pl.BlockSpec((1, tm, WIDTH), row),
                   pl.BlockSpec((1, tm, WIDTH), row),
                   pl.BlockSpec((1, tm, WIDTH), row),
                   pl.BlockSpec((1, tm, GATE_COLS), row)],
        out_shape=[jax.ShapeDtypeStruct((bn, s, RWKV_COLS), f32),
                   jax.ShapeDtypeStruct((bn, s, WIDTH), bf16),
                   jax.ShapeDtypeStruct((bn, s, WIDTH), bf16),
                   jax.ShapeDtypeStruct((bn, s, WIDTH), bf16),
                   jax.ShapeDtypeStruct((bn, s, GATE_COLS), f32)],
        compiler_params=_cparams(("parallel", "parallel")),
        name="inproj",
    )(x, gain.reshape(1, d), shift.reshape(bn, 1, d), scale.reshape(bn, 1, d), w_bf,
      per_head(q_g, HEAD_DIM ** -0.5), per_head(k_g, 1.0), ones_bd)


def _split_bf16(x):
    hi = x.astype(bf16)
    return hi, (x - hi.astype(f32)).astype(bf16)


def _mm_split(x, w2):
    rows, n = x.shape[0], w2.shape[1] // 2
    pr = jnp.dot(jnp.concatenate(_split_bf16(x), axis=0), w2, preferred_element_type=f32)
    return (pr[:rows, :n] + pr[rows:, :n]) + (pr[:rows, n:] + pr[rows:, n:])


def _bmm(a, b):
    return jnp.dot(a, b, preferred_element_type=f32)


def _bmm_nt(a, b):
    return lax.dot_general(a, b, (((1,), (1,)), ((), ())), preferred_element_type=f32)


def _bmm_tn(a, b):
    return lax.dot_general(a, b, (((0,), (0,)), ((), ())), preferred_element_type=f32)


def _rwkv_kernel(p_ref, mu_ref, w0_ref, wup_ref, a0_ref, aup_ref, gup_ref, kk_ref, ka_ref,
                 rk_ref, lg_ref, lb_ref, o_ref, last_ref, st_ref):
    C = RWKV_CHUNK
    c_idx = pl.program_id(1)

    @pl.when(c_idx == 0)
    def _():
        last_ref[...] = jnp.zeros_like(last_ref)
        st_ref[...] = jnp.zeros_like(st_ref)

    NB = p_ref.shape[0]
    R = NB * C
    rows_of = [slice(b * C, (b + 1) * C) for b in range(NB)]
    pc = p_ref[...].reshape(R, RWKV_COLS)
    row_full = lax.broadcasted_iota(jnp.int32, pc.shape, 0)
    prev = pltpu.roll(pc, 1, axis=0)
    for b in range(NB):
        prev = jnp.where(row_full == b * C, last_ref[b:b + 1, :], prev)
        last_ref[b:b + 1, :] = pc[(b + 1) * C - 1:(b + 1) * C, :]
    ps = pc + (prev - pc) * mu_ref[...]

    o1, o2, o3 = WIDTH, 2 * WIDTH, 3 * WIDTH
    o4, o5 = o3 + DECAY_LORA, o3 + DECAY_LORA + ICLR_LORA
    r, k, v = ps[:, :o1], ps[:, o1:o2], ps[:, o2:o3]
    wd, ad, gd = ps[:, o3:o4], ps[:, o4:o5], ps[:, o5:]

    wraw = w0_ref[...] + _mm_split(jnp.tanh(wd), wup_ref[...])
    z = -wraw
    softplus = jnp.maximum(z, 0.0) + jnp.log(1.0 + jnp.exp(-jnp.abs(z)))
    logdec = -jnp.exp(-softplus - 0.5)
    a = _sigmoid(a0_ref[...] + _mm_split(ad, aup_ref[...]))
    g = _mm_split(_sigmoid(gd), gup_ref[...])
    kkr = k * kk_ref[...]
    k2 = k * (1.0 + (a - 1.0) * ka_ref[...])
    rkk = r * k2 * rk_ref[...]

    ri = lax.broadcasted_iota(jnp.int32, (R, R), 0)
    ci = lax.broadcasted_iota(jnp.int32, (R, R), 1)
    tri = ((ri >= ci) & (ri // C == ci // C)).astype(bf16)
    ld_hi = logdec.astype(bf16)
    ld_r = logdec - ld_hi.astype(f32)
    ld_mid = ld_r.astype(bf16)
    ld_lo = (ld_r - ld_mid.astype(f32)).astype(bf16)
    cum3 = _bmm(tri, jnp.concatenate([ld_hi, ld_mid, ld_lo], axis=1))
    cum = cum3[:, :WIDTH] + cum3[:, WIDTH:2 * WIDTH] + cum3[:, 2 * WIDTH:]
    e_pos = jnp.exp(cum)
    e_prev = jnp.exp(cum - logdec)
    e_neg = jnp.exp(-cum)
    cum_ends = [cum[(b + 1) * C - 1:(b + 1) * C, :] for b in range(NB)]
    e_end = jnp.exp(jnp.concatenate([jnp.broadcast_to(ce, (C, WIDTH)) for ce in cum_ends], axis=0)
                    - cum)
    g_end = [jnp.exp(ce) for ce in cum_ends]

    C2 = 2 * C
    lane1 = lax.broadcasted_iota(jnp.int32, (1, 128), 1)
    keep_lo = (lane1 < HEAD_DIM).astype(bf16)
    keep_hi = (lane1 >= HEAD_DIM).astype(bf16)

    def bd(x):
        return jnp.concatenate([x * keep_lo, x * keep_hi], axis=0)

    r2 = lax.broadcasted_iota(jnp.int32, (C2, C2), 0)
    c2 = lax.broadcasted_iota(jnp.int32, (C2, C2), 1)
    ones_bd = ((r2 // HEAD_DIM) == (c2 // HEAD_DIM)).astype(bf16)
    eye = (r2 == c2).astype(f32)
    rr, cc = r2 % C, c2 % C
    strict = rr > cc
    blk16 = (rr // 16) == (cc // 16)
    blk32 = (rr // 32) == (cc // 32)
    m_diag16 = strict & blk16
    m_off32 = strict & blk32 & jnp.logical_not(blk16)
    m_off64 = strict & jnp.logical_not(blk32)
    r4 = lax.broadcasted_iota(jnp.int32, (C2, 2 * C2), 0) % C
    c4 = lax.broadcasted_iota(jnp.int32, (C2, 2 * C2), 1) % C
    strict2 = r4 > c4
    incl2 = r4 >= c4
    zeros_bd = jnp.zeros((C2, C2), bf16)

    P = range(N_HEADS // 2)
    ps_ = [slice(p * 128, (p + 1) * 128) for p in P]

    def head_sums(x):
        return jnp.concatenate([_bmm(x[:, s], ones_bd) for s in ps_], axis=1)

    kkn = kkr * lax.rsqrt(jnp.maximum(head_sums((kkr * kkr).astype(bf16)), 1e-24))
    bh = kkn * a
    at_w = (-kkn * e_prev).astype(bf16)
    rt_w = (r * e_pos).astype(bf16)
    bt_w = (bh * e_neg).astype(bf16)
    kt_w = (k2 * e_neg).astype(bf16)
    bhat_w = (bh * e_end).astype(bf16)
    khat_w = (k2 * e_end).astype(bf16)
    v_w = v.astype(bf16)

    units = [(b, p) for b in range(NB) for p in P]
    P = range(len(units))

    def sub(x, q):
        return x[rows_of[units[q][0]], ps_[units[q][1]]]

    at = [bd(sub(at_w, q)) for q in P]
    rt = [bd(sub(rt_w, q)) for q in P]
    vb = [bd(sub(v_w, q)) for q in P]
    smat = [_bmm_nt(jnp.concatenate([at[p], rt[p]], axis=0),
                    jnp.concatenate([bd(sub(bt_w, p)), bd(sub(kt_w, p))], axis=0)) for p in P]
    n_aak = [jnp.where(strict2, smat[p][:C2], 0.0) for p in P]
    arb_ark = [jnp.where(incl2, smat[p][C2:], 0.0).astype(bf16) for p in P]
    nmat = [n_aak[p][:, :C2] for p in P]

    pw = [jnp.where(m_diag16, nmat[p], 0.0) for p in P]
    tinv = [eye + pw[p] for p in P]
    for _ in range(3):
        pwb = [pw[p].astype(bf16) for p in P]
        pw = [_bmm(pwb[p], pwb[p]) for p in P]
        tinv = [tinv[p] + _bmm(tinv[p].astype(bf16), pw[p].astype(bf16)) for p in P]
    for m_off in (m_off32, m_off64):
        tb = [tinv[p].astype(bf16) for p in P]
        mid = [_bmm(jnp.where(m_off, nmat[p], 0.0).astype(bf16), tb[p]).astype(bf16) for p in P]
        tinv = [tinv[p] + _bmm(tb[p], mid[p]) for p in P]
    tb = [tinv[p].astype(bf16) for p in P]

    aakv = [_bmm(n_aak[p][:, C2:].astype(bf16), vb[p]).astype(bf16) for p in P]
    m1u0 = [_bmm(tb[p], jnp.concatenate([at[p], aakv[p]], axis=1)).astype(bf16) for p in P]
    rhs = [jnp.concatenate([m1u0[p], jnp.concatenate([zeros_bd, vb[p]], axis=1)], axis=0)
           for p in P]
    my = [_bmm(arb_ark[p], rhs[p]) for p in P]
    gd_ = [_bmm_tn(jnp.concatenate([bd(sub(bhat_w, p)), bd(sub(khat_w, p))], axis=0), rhs[p])
           for p in P]
    m2 = [(rt[p].astype(f32) + my[p][:, :C2]).astype(bf16) for p in P]
    gmat = [(eye * g_end[units[p][0]][:, ps_[units[p][1]]] + gd_[p][:, :C2]).astype(bf16)
            for p in P]
    upd = [_bmm(jnp.concatenate([m2[p], gmat[p]], axis=0), st_ref[p].astype(bf16)) for p in P]

    lane_c = lax.broadcasted_iota(jnp.int32, (C, 128), 1)
    ys = []
    for p in P:
        st_ref[p] = upd[p][C2:] + gd_[p][:, C2:]
        y_bd = upd[p][:C2] + my[p][:, C2:]
        ys.append(jnp.where(lane_c < HEAD_DIM, y_bd[:C], y_bd[C:]))
    n_pairs = len(ps_)
    y = jnp.concatenate([jnp.concatenate(ys[b * n_pairs:(b + 1) * n_pairs], axis=1)
                         for b in range(NB)], axis=0)

    y_hi = y.astype(bf16)
    y_lo = (y - y_hi.astype(f32)).astype(bf16)
    sums = head_sums(jnp.concatenate([y_hi, y_lo, rkk.astype(bf16)], axis=0))
    yc = y - (sums[:R] + sums[R:2 * R]) * (1.0 / HEAD_DIM)
    var = head_sums((yc * yc).astype(bf16)) * (1.0 / HEAD_DIM)
    yn = yc * lax.rsqrt(var + GN_EPS) * lg_ref[...] + lb_ref[...]
    o_ref[...] = ((yn + sums[2 * R:] * v) * g).reshape(NB, C, WIDTH)


def _rwkv(p_rwkv, mu, w0, w_up, a0, a_up, g_up, k_k, k_a, r_k, lnx_g, lnx_b):
    bn, s, _ = p_rwkv.shape
    C = RWKV_CHUNK
    vec = lambda t: t.reshape(1, -1)
    const = lambda b, c: (0, 0)
    full = lambda t: pl.BlockSpec(t.shape, const)
    hi_lo = lambda w: jnp.concatenate(_split_bf16(w.astype(f32)), axis=1)
    args = [vec(mu), vec(w0), hi_lo(w_up), vec(a0), hi_lo(a_up), hi_lo(g_up), vec(k_k), vec(k_a),
            vec(r_k), vec(lnx_g), vec(lnx_b)]
    nb = RWKV_BATCH_ROWS if bn % RWKV_BATCH_ROWS == 0 else 1
    return pl.pallas_call(
        _rwkv_kernel,
        grid=(bn // nb, s // C),
        in_specs=[pl.BlockSpec((nb, C, RWKV_COLS), lambda b, c: (b, c, 0))] + [full(t) for t in args],
        out_specs=pl.BlockSpec((nb, C, WIDTH), lambda b, c: (b, c, 0)),
        out_shape=jax.ShapeDtypeStruct((bn, s, WIDTH), f32),
        scratch_shapes=[pltpu.VMEM((nb, RWKV_COLS), f32),
                        pltpu.VMEM((nb * N_HEADS // 2, 2 * HEAD_DIM, 2 * HEAD_DIM), f32)],
        compiler_params=_cparams(("parallel", "arbitrary")),
        name="rwkv",
    )(p_rwkv, *args)


def _attn_kernel(q_ref, k_ref, v_ref, bias_ref, o_ref):
    TQ = ATT_QROWS
    t = pl.program_id(1)
    g0 = pl.multiple_of(jnp.maximum(t - PREV_CHUNKS * CHUNK // TQ, 0) * TQ, TQ)
    q = q_ref[0]
    lane = lax.broadcasted_iota(jnp.int32, (1, 128), 1)
    keep_lo = (lane < HEAD_DIM).astype(bf16)
    keep_hi = (lane >= HEAD_DIM).astype(bf16)
    lane_f = lax.broadcasted_iota(jnp.int32, (TQ, 128), 1)
    pairs = range(N_HEADS // 2)

    scores = []
    for p in pairs:
        qp = q[:, p * 128:(p + 1) * 128]
        qs = jnp.concatenate([qp * keep_lo, qp * keep_hi], axis=0)
        kp = k_ref[0, pl.ds(g0, ATT_WINDOW), p * 128:(p + 1) * 128]
        s = lax.dot_general(qs, kp, (((1,), (1,)), ((), ())), preferred_element_type=f32)
        scores.append(s + jnp.concatenate([bias_ref[0, 2 * p], bias_ref[0, 2 * p + 1]], axis=0))
    probs, denoms = [], []
    for p in pairs:
        m = jnp.max(scores[p], axis=-1, keepdims=True)
        e = jnp.exp(scores[p] - m)
        denoms.append(jnp.sum(e, axis=-1, keepdims=True))
        probs.append(e.astype(bf16))
    outs = []
    for p in pairs:
        vp = v_ref[0, pl.ds(g0, ATT_WINDOW), p * 128:(p + 1) * 128]
        o2 = jnp.dot(probs[p], vp, preferred_element_type=f32) / denoms[p]
        outs.append(jnp.where(lane_f < HEAD_DIM, o2[:TQ], o2[TQ:]))
    o_ref[0] = jnp.concatenate(outs, axis=-1)


def _attn_bias_tables(rel_bias):
    TQ, W = ATT_QROWS, ATT_WINDOW
    n_var = PREV_CHUNKS * CHUNK // TQ + 1
    n_h = rel_bias.shape[0]
    rb = rel_bias.astype(f32)
    ext = jnp.concatenate([jnp.broadcast_to(rb[:, :1], (n_h, W - REL_CLIP)), rb,
                           jnp.broadcast_to(rb[:, -1:], (n_h, W - REL_CLIP - 1))], axis=1)
    width = W + (n_var - 1) * TQ
    toep = jnp.stack([ext[:, TQ - qi: TQ - qi + width] for qi in range(TQ)], axis=1)
    qc = np.arange(TQ)[:, None] // CHUNK
    kc = np.arange(W)[None, :] // CHUNK
    tables = []
    for v in range(n_var):
        off = (n_var - 1 - v) * TQ
        rel_chunk = kc - qc - v * (TQ // CHUNK)
        visible = (rel_chunk <= 0) & (rel_chunk >= -PREV_CHUNKS)
        tables.append(jnp.where(jnp.asarray(visible)[None], toep[:, :, off: off + W], NEG_INF))
    return jnp.stack(tables, axis=0)


def _attn(q, k, v, rel_bias):
    bn, s, _ = q.shape
    TQ = ATT_QROWS
    bias = _attn_bias_tables(rel_bias)
    n_var = bias.shape[0]
    tile = pl.BlockSpec((1, TQ, WIDTH), lambda b, t: (b, t, 0))
    whole = pl.BlockSpec((1, s, WIDTH), lambda b, t: (b, 0, 0))
    return pl.pallas_call(
        _attn_kernel,
        grid=(bn, s // TQ),
        in_specs=[tile, whole, whole,
                  pl.BlockSpec((1, N_HEADS, TQ, ATT_WINDOW),
                               lambda b, t: (jnp.minimum(t, n_var - 1), 0, 0, 0))],
        out_specs=tile,
        out_shape=jax.ShapeDtypeStruct((bn, s, WIDTH), f32),
        compiler_params=_cparams(("parallel", "arbitrary")),
        name="attn",
    )(q, k, v, bias)


def _store_tile_rows(ref, x):
    rows = x.shape[0]
    for j in range(ROW_TILES):
        ref[pl.ds(j, rows, stride=ROW_TILES), :] = x[:, j * 128:(j + 1) * 128]


def _load_tile_rows(ref, rows):
    return jnp.concatenate(
        [ref[pl.ds(j, rows, stride=ROW_TILES), :] for j in range(ROW_TILES)], axis=1)


def _route(logits, tri, counts):
    lane = lax.broadcasted_iota(jnp.int32, logits.shape, 1).astype(f32)
    big = float(ROUTE_LANES)
    is_c = lane < N_GROUPS
    cl = jnp.where(is_c, logits, NEG_INF)
    cm = jnp.max(cl, axis=-1, keepdims=True)
    grp = jnp.min(jnp.where(cl == cm, lane, big), axis=-1, keepdims=True)
    p_grp = 1.0 / jnp.sum(jnp.where(is_c, jnp.exp(cl - cm), 0.0), axis=-1, keepdims=True)
    lo = N_GROUPS + grp * EXPERTS_PER_GROUP
    in_g = (lane >= lo) & (lane < lo + EXPERTS_PER_GROUP)
    fl = jnp.where(in_g, logits, NEG_INF)
    f1 = jnp.max(fl, axis=-1, keepdims=True)
    i1 = jnp.min(jnp.where(in_g & (fl == f1), lane, big), axis=-1, keepdims=True)
    fl2 = jnp.where(lane == i1, NEG_INF, fl)
    in_g2 = in_g & (lane != i1)
    f2 = jnp.max(fl2, axis=-1, keepdims=True)
    i2 = jnp.min(jnp.where(in_g2 & (fl2 == f2), lane, big), axis=-1, keepdims=True)
    e2 = jnp.exp(f2 - f1)
    w1 = p_grp / (1.0 + e2)
    w2 = p_grp * e2 / (1.0 + e2)

    hot1 = (lane == i1).astype(f32)
    hot2 = (lane == i2).astype(f32)
    pre1 = jnp.dot(tri, hot1.astype(bf16), preferred_element_type=f32)
    pre2 = jnp.dot(tri, hot2.astype(bf16), preferred_element_type=f32)
    tot1 = jnp.sum(hot1, axis=0, keepdims=True)
    tot2 = jnp.sum(hot2, axis=0, keepdims=True)
    rank1 = jnp.sum(hot1 * (pre1 + counts), axis=-1, keepdims=True)
    rank2 = jnp.sum(hot2 * (pre2 + counts + tot1), axis=-1, keepdims=True)

    out = jnp.where(lane == 0, w1, 0.0)
    out = jnp.where(lane == 1, w2, out)
    out = jnp.where(lane == 2, i1 - N_GROUPS, out)
    out = jnp.where(lane == 3, i2 - N_GROUPS, out)
    out = jnp.where(lane == 4, rank1, out)
    out = jnp.where(lane == 5, rank2, out)
    return out, counts + tot1 + tot2


def _merge_kernel(yr_ref, ya_ref, gt_ref, x_ref, g1_ref, sh_ref, sc_ref, ng_ref, wbr_ref,
                  wba_ref, wout_ref, wrt_ref, brt_ref, tri_ref, x1_ref, h2_ref, route_ref,
                  cnt_ref):
    @pl.when((pl.program_id(0) == 0) & (pl.program_id(1) == 0))
    def _():
        cnt_ref[...] = jnp.zeros_like(cnt_ref)

    t1 = jnp.dot(yr_ref[0].astype(bf16), wbr_ref[...], preferred_element_type=f32)
    t2 = jnp.dot(ya_ref[0].astype(bf16), wba_ref[...], preferred_element_type=f32)
    gt = gt_ref[0]
    m = gt[:, :D_MODEL] * t1 + gt[:, D_MODEL:] * t2
    mixed = jnp.dot(m.astype(bf16), wout_ref[...], preferred_element_type=f32)
    x1 = x_ref[0] + g1_ref[0] * mixed
    x1_ref[0] = x1
    h2 = _rms_mod(x1, ng_ref[...], sh_ref[0], sc_ref[0])
    _store_tile_rows(h2_ref.at[0], h2)
    h2_hi = h2.astype(bf16)
    h2_lo = (h2 - h2_hi.astype(f32)).astype(bf16)
    pa = jnp.dot(h2_hi, wrt_ref[...], preferred_element_type=f32)
    pb = jnp.dot(h2_lo, wrt_ref[...], preferred_element_type=f32)
    logits = ((pa[:, :ROUTE_LANES] + pb[:, :ROUTE_LANES])
              + (pa[:, ROUTE_LANES:] + pb[:, ROUTE_LANES:]) + brt_ref[...])
    route, counts = _route(logits, tri_ref[...], cnt_ref[...])
    route_ref[0] = route
    cnt_ref[...] = counts


def _merge(y_r, y_a, gates, x, g1, sh2, sc2, norm2_g, w_br, w_ba, w_out, w_rt, b_rt):
    bn, s, d = x.shape
    tm = min(MERGE_ROWS, s)
    row = lambda b, i: (b, i, 0)
    per_b = lambda b, i: (b, 0, 0)
    const = lambda b, i: (0, 0)
    full = lambda t: pl.BlockSpec(t.shape, const)
    tri = jnp.asarray(np.tril(np.ones((tm, tm), np.float32), -1), bf16)
    return pl.pallas_call(
        _merge_kernel,
        grid=(bn, s // tm),
        in_specs=[pl.BlockSpec((1, tm, WIDTH), row),
                  pl.BlockSpec((1, tm, WIDTH), row),
                  pl.BlockSpec((1, tm, GATE_COLS), row),
                  pl.BlockSpec((1, tm, d), row),
                  pl.BlockSpec((1, 1, d), per_b),
                  pl.BlockSpec((1, 1, d), per_b),
                  pl.BlockSpec((1, 1, d), per_b),
                  pl.BlockSpec((1, d), const),
                  full(w_br), full(w_ba), full(w_out), full(w_rt), full(b_rt), full(tri)],
        out_specs=[pl.BlockSpec((1, tm, d), row),
                   pl.BlockSpec((1, tm * ROW_TILES, 128), row),
                   pl.BlockSpec((1, tm, ROUTE_LANES), row),
                   pl.BlockSpec((1, ROUTE_LANES), const)],
        out_shape=[jax.ShapeDtypeStruct((bn, s, d), f32),
                   jax.ShapeDtypeStruct((bn, s * ROW_TILES, 128), f32),
                   jax.ShapeDtypeStruct((bn, s, ROUTE_LANES), f32),
                   jax.ShapeDtypeStruct((1, ROUTE_LANES), f32)],
        compiler_params=_cparams(("arbitrary", "arbitrary")),
        name="merge",
    )(y_r, y_a, gates, x, g1.reshape(bn, 1, d), sh2.reshape(bn, 1, d), sc2.reshape(bn, 1, d),
      norm2_g.reshape(1, d), w_br, w_ba, w_out, w_rt, b_rt, tri)


def _row_copy(src, src_row, dst, dst_row, sem):
    return pltpu.make_async_copy(src.at[src_row], dst.at[pl.ds(dst_row * ROW_TILES, ROW_TILES)],
                                 sem)


def _moe_kernel(tok_ref, be_ref, j0_ref, nv_ref, h2_hbm, wg_ref, wu_ref, wd_ref, o_ref, xbuf, sem):
    del be_ref
    i = pl.program_id(0)
    last = pl.num_programs(0) - 1

    def chunks(blk, fn):
        for c in range(MOE_ROWS // MOE_CHUNK):
            @pl.when(c * MOE_CHUNK < nv_ref[blk])
            def _(c=c):
                for r in range(c * MOE_CHUNK, (c + 1) * MOE_CHUNK):
                    fn(r)

    def issue(blk, slot):
        j0 = j0_ref[blk]
        chunks(blk, lambda r: _row_copy(h2_hbm, tok_ref[j0 + r], xbuf.at[slot], r,
                                        sem.at[slot]).start(priority=r % 2))

    def wait(blk, slot):
        chunks(blk, lambda r: _row_copy(h2_hbm, 0, xbuf.at[slot], r, sem.at[slot]).wait())

    n_slots = xbuf.shape[0]
    ahead = n_slots - 1

    @pl.when(i == 0)
    def _():
        xbuf[...] = jnp.zeros_like(xbuf)
        for b in range(ahead):
            @pl.when(b <= last)
            def _(b=b):
                issue(b, b)

    for slot in range(n_slots):
        @pl.when(i % n_slots == slot)
        def _(slot=slot):
            wait(i, slot)

            @pl.when(i + ahead <= last)
            def _():
                issue(i + ahead, (slot + ahead) % n_slots)

            @pl.when(nv_ref[i] > 0)
            def _():
                xb = _load_tile_rows(xbuf.at[slot], MOE_ROWS).astype(bf16)
                gate = jnp.dot(xb, wg_ref[0].astype(bf16), preferred_element_type=f32)
                up = jnp.dot(xb, wu_ref[0].astype(bf16), preferred_element_type=f32)
                hid = gate * _sigmoid(gate) * up
                _store_tile_rows(
                    o_ref, jnp.dot(hid.astype(bf16), wd_ref[0].astype(bf16),
                                   preferred_element_type=f32))

            @pl.when(nv_ref[i] == 0)
            def _():
                o_ref[...] = jnp.zeros_like(o_ref)


def _moe(h2_tiles, tok_sorted, block_e, block_j0, block_nv, wg, wu, wd):
    d = D_MODEL
    n_blocks = block_e.shape[0]
    blk_rows = MOE_ROWS * ROW_TILES
    by_expert = lambda i, tok, be, j0, nv: (be[i], 0, 0)
    grid_spec = pltpu.PrefetchScalarGridSpec(
        num_scalar_prefetch=4,
        grid=(n_blocks,),
        in_specs=[pl.BlockSpec(memory_space=pl.ANY),
                  pl.BlockSpec((1, d, D_EXPERT), by_expert),
                  pl.BlockSpec((1, d, D_EXPERT), by_expert),
                  pl.BlockSpec((1, D_EXPERT, d), by_expert)],
        out_specs=pl.BlockSpec((blk_rows, 128), lambda i, tok, be, j0, nv: (i, 0)),
        scratch_shapes=[pltpu.VMEM((MOE_SLOTS, blk_rows, 128), f32),
                        pltpu.SemaphoreType.DMA((MOE_SLOTS,))],
    )
    yb = pl.pallas_call(
        _moe_kernel,
        grid_spec=grid_spec,
        out_shape=jax.ShapeDtypeStruct((n_blocks * blk_rows, 128), f32),
        compiler_params=_cparams(("arbitrary",)),
        name="moe",
    )(tok_sorted, block_e, block_j0, block_nv, h2_tiles, wg, wu, wd)
    return yb.reshape(n_blocks * MOE_ROWS, ROW_TILES, 128)


def _combine_kernel(d0_ref, d1_ref, yb_hbm, x1_ref, rt_ref, g2_ref, o_ref, buf, sem):
    tc = buf.shape[2] // ROW_TILES
    step = pl.program_id(0) * pl.num_programs(1) + pl.program_id(1)
    n_steps = pl.num_programs(0) * pl.num_programs(1)

    def issue(for_step, slot):
        base = for_step * tc
        for r in range(tc):
            _row_copy(yb_hbm, d0_ref[base + r], buf.at[slot, 0], r, sem.at[slot]).start(priority=0)
            _row_copy(yb_hbm, d1_ref[base + r], buf.at[slot, 1], r, sem.at[slot]).start(priority=1)

    def wait(slot):
        for r in range(tc):
            _row_copy(yb_hbm, 0, buf.at[slot, 0], r, sem.at[slot]).wait()
            _row_copy(yb_hbm, 0, buf.at[slot, 1], r, sem.at[slot]).wait()

    @pl.when(step == 0)
    def _():
        issue(step, 0)

    for slot in range(2):
        @pl.when(step % 2 == slot)
        def _(slot=slot):
            @pl.when(step + 1 < n_steps)
            def _():
                issue(step + 1, 1 - slot)

            wait(slot)
            rt = rt_ref[0]
            moe = (rt[:, 0:1] * _load_tile_rows(buf.at[slot, 0], tc)
                   + rt[:, 1:2] * _load_tile_rows(buf.at[slot, 1], tc))
            o_ref[0] = x1_ref[0] + g2_ref[0] * moe


def _combine(dest0, dest1, yb, x1, route, g2):
    bn, s, d = x1.shape
    tc = min(COMBINE_ROWS, s)
    row = lambda b, i, d0, d1: (b, i, 0)
    grid_spec = pltpu.PrefetchScalarGridSpec(
        num_scalar_prefetch=2,
        grid=(bn, s // tc),
        in_specs=[pl.BlockSpec(memory_space=pl.ANY),
                  pl.BlockSpec((1, tc, d), row),
                  pl.BlockSpec((1, tc, ROUTE_LANES), row),
                  pl.BlockSpec((1, 1, d), lambda b, i, d0, d1: (b, 0, 0))],
        out_specs=pl.BlockSpec((1, tc, d), row),
        scratch_shapes=[pltpu.VMEM((2, 2, tc * ROW_TILES, 128), f32),
                        pltpu.SemaphoreType.DMA((2,))],
    )
    return pl.pallas_call(
        _combine_kernel,
        grid_spec=grid_spec,
        out_shape=jax.ShapeDtypeStruct((bn, s, d), f32),
        compiler_params=_cparams(("arbitrary", "arbitrary")),
        name="combine",
    )(dest0, dest1, yb, x1, route, g2.reshape(bn, 1, d))


def _dispatch_plan(route, counts):
    n_tok = route.shape[0]
    counts = counts.astype(jnp.int32)
    padded = (counts + MOE_ROWS - 1) // MOE_ROWS * MOE_ROWS
    pad_ends = jnp.cumsum(padded)
    pad_starts = pad_ends - padded
    starts = jnp.cumsum(counts) - counts
    n_blocks = (2 * n_tok + N_EXPERTS * MOE_ROWS) // MOE_ROWS
    experts = jnp.arange(N_EXPERTS, dtype=jnp.int32)
    dest = []
    for k in range(2):
        eid = route[:, 2 + k].astype(jnp.int32)
        rank = route[:, 4 + k].astype(jnp.int32)
        dest.append(rank + jnp.sum(jnp.where(eid[:, None] == experts, pad_starts, 0), axis=-1))
    tok = jnp.arange(n_tok, dtype=jnp.int32)
    _, tok_sorted = lax.sort((jnp.concatenate(dest), jnp.concatenate([tok, tok])), num_keys=1)
    tok_sorted = jnp.concatenate([tok_sorted, jnp.zeros((MOE_ROWS,), jnp.int32)])
    block_row = jnp.arange(n_blocks, dtype=jnp.int32) * MOE_ROWS
    block_e = jnp.minimum(jnp.sum(pad_ends[None, :] <= block_row[:, None], axis=-1),
                          N_EXPERTS - 1).astype(jnp.int32)
    hot_b = block_e[:, None] == experts
    block_j0 = block_row + jnp.sum(jnp.where(hot_b, starts - pad_starts, 0), axis=-1)
    block_j0 = jnp.clip(block_j0, 0, 2 * n_tok - 1).astype(jnp.int32)
    block_nv = jnp.clip(jnp.sum(jnp.where(hot_b, counts + pad_starts, 0), axis=-1) - block_row,
                        0, MOE_ROWS).astype(jnp.int32)
    return tok_sorted, block_e, block_j0, block_nv, dest[0], dest[1]


def kernel(x, c, w_ada, b_ada, norm1_g, w_in, rwkv_mu, rwkv_w0, rwkv_w_up, rwkv_a0, rwkv_a_up, rwkv_g_up, rwkv_k_k, rwkv_k_a, rwkv_r_k, rwkv_lnx_g, rwkv_lnx_b, attn_q_g, attn_k_g, attn_rel_bias, w_branch_rwkv, w_branch_attn, w_out, norm2_g, router_coarse_w, router_coarse_b, router_fine_w, router_fine_b, expert_w_gate, expert_w_up, expert_w_down):
    bn, s, d = x.shape
    depth = w_ada.shape[0]
    for l in range(depth):
        mod = _ada(c, w_ada[l], b_ada[l])
        sh1, sc1, g1, sh2, sc2, g2 = [mod[:, i * d:(i + 1) * d] for i in range(6)]

        p_rwkv, q, k, v, gates = _inproj(x, norm1_g[l], sh1, sc1, w_in[l].astype(bf16),
                                         attn_q_g[l], attn_k_g[l])
        y_r = _rwkv(p_rwkv, rwkv_mu[l], rwkv_w0[l], rwkv_w_up[l], rwkv_a0[l], rwkv_a_up[l],
                    rwkv_g_up[l], rwkv_k_k[l], rwkv_k_a[l], rwkv_r_k[l].reshape(-1),
                    rwkv_lnx_g[l], rwkv_lnx_b[l])
        y_a = _attn(q, k, v, attn_rel_bias[l])

        n_rt = N_GROUPS + N_EXPERTS
        w_rt = jnp.zeros((d, ROUTE_LANES), f32)
        w_rt = w_rt.at[:, :N_GROUPS].set(router_coarse_w[l]).at[:, N_GROUPS:n_rt].set(router_fine_w[l])
        w_rt_hi = w_rt.astype(bf16)
        w_rt = jnp.concatenate([w_rt_hi, (w_rt - w_rt_hi.astype(f32)).astype(bf16)], axis=1)
        b_rt = jnp.zeros((1, ROUTE_LANES), f32)
        b_rt = b_rt.at[0, :N_GROUPS].set(router_coarse_b[l]).at[0, N_GROUPS:n_rt].set(router_fine_b[l])
        x1, h2, route, counts = _merge(y_r, y_a, gates, x, g1, sh2, sc2, norm2_g[l],
                                       w_branch_rwkv[l].astype(bf16),
                                       w_branch_attn[l].astype(bf16),
                                       w_out[l].astype(bf16), w_rt, b_rt)

        tok_sorted, block_e, block_j0, block_nv, dest0, dest1 = _dispatch_plan(
            route.reshape(bn * s, ROUTE_LANES), counts[0, N_GROUPS:n_rt])
        yb = _moe(h2.reshape(bn * s, ROW_TILES, 128), tok_sorted, block_e, block_j0, block_nv,
                  expert_w_gate[l], expert_w_up[l], expert_w_down[l])
        x = _combine(dest0, dest1, yb, x1, route, g2)
    return x
```
